```python
import math
import jax, jax.numpy as jnp
from jax import lax
import numpy as np

D_MODEL = 1024
BATCH = 4
SEQ = 4096
DEPTH = 4
DEC_BATCH = 128
DEC_SEQ = 4
PAST_LEN = 2048
PAGE_SIZE = 128

N_MIXERS = 3
N_A = (DEPTH + 2) // 3
N_B = (DEPTH + 1) // 3
N_C = DEPTH // 3
DEEPNORM_ALPHA = (2 * DEPTH) ** 0.25
DEEPNORM_BETA = (8 * DEPTH) ** -0.25
LN_EPS = 1e-5
NORM_EPS = 1e-6
D_FF = 2816
PLE_DIM = 256
A_CHUNK = 128
A_INNER = 2 * D_MODEL
A_GROUPS = 16
A_GROUP_DIM = A_INNER // A_GROUPS
GDN_H = 8
GDN_DK = D_MODEL // GDN_H
GDN_DV = D_MODEL // GDN_H
GDN_WK = GDN_H * GDN_DK
GDN_WV = GDN_H * GDN_DV
GDN_QKV = 2 * GDN_WK + GDN_WV
GDN_PROJ = GDN_QKV + GDN_WV + 2 * GDN_H
GDN_CONV = 4
GDN_CHUNK = 64
NSA_H = 16
NSA_DH = D_MODEL // NSA_H
NSA_G = 4
NSA_HG = NSA_H // NSA_G
NSA_L_CMP = 32
NSA_L_SLC = 64
NSA_CMP_PER_SLC = NSA_L_SLC // NSA_L_CMP
NSA_N_SEL = 16
NSA_WINDOW = 512
NSA_WIN_QBLK = 128
NSA_SEL_ROWS = 128
NSA_CMP_HID = 256
NSA_Q = NSA_H * NSA_DH
NSA_KV = NSA_G * NSA_DH
NSA_PROJ = NSA_Q + 6 * NSA_KV + 3 * NSA_H
NSA_FORCE = 100.0
NEG_INF = -1e30
NUM_BUCKETS = 32
MAX_DISTANCE = 128

kernel_name = 'hybrid_chunkmlp_gdn_nsa_step'


def layer_norm(x, g, b):
    xf = x.astype(jnp.float32)
    mu = xf.mean(-1, keepdims=True)
    var = jnp.square(xf - mu).mean(-1, keepdims=True)
    y = (xf - mu) * lax.rsqrt(var + LN_EPS) * g.astype(jnp.float32) + b.astype(jnp.float32)
    return y.astype(x.dtype)


def rms_norm(x, g):
    xf = x.astype(jnp.float32)
    return xf * lax.rsqrt(jnp.square(xf).mean(-1, keepdims=True) + NORM_EPS) * g.astype(jnp.float32)


def l2_normalize(x):
    xf = x.astype(jnp.float32)
    return (xf * lax.rsqrt(jnp.square(xf).sum(-1, keepdims=True) + NORM_EPS)).astype(x.dtype)


def post_norm(x, y, g, b):
    return layer_norm(DEEPNORM_ALPHA * x + y, g, b)


def swiglu(x, w_up, w_down):
    a, b = jnp.split(x @ w_up, 2, axis=-1)
    return (jax.nn.silu(a) * b) @ w_down


def masked_softmax(logits, mask, axis=-1):
    p = jax.nn.softmax(jnp.where(mask, logits, NEG_INF), axis=axis)
    return jnp.where(mask, p, 0.0)


def t5_bucket(dist):
    n = jnp.maximum(dist, 0)
    max_exact = NUM_BUCKETS // 2
    nf = jnp.maximum(n, 1).astype(jnp.float32)
    large = max_exact + (jnp.log(nf / max_exact) / math.log(MAX_DISTANCE / max_exact)
                         * (NUM_BUCKETS - max_exact)).astype(jnp.int32)
    large = jnp.minimum(large, NUM_BUCKETS - 1)
    return jnp.where(n < max_exact, n, large)


def t5_bias(tab, dist):
    return tab.astype(jnp.float32)[t5_bucket(dist)]


def chunk_mlp_mixer(x, w_in, ln_g, ln_b, w_s, b_s, w_out):
    B, T, _ = x.shape
    h = jax.nn.gelu(x @ w_in)
    u, v = h[..., :A_INNER], h[..., A_INNER:]
    v = layer_norm(v, ln_g, ln_b)
    n_chunks = -(-T // A_CHUNK)
    vb = jnp.pad(v, ((0, 0), (0, n_chunks * A_CHUNK - T), (0, 0)))
    vb = vb.reshape(B, n_chunks, A_CHUNK, A_GROUPS, A_GROUP_DIM)
    causal = jnp.tril(jnp.ones((A_CHUNK, A_CHUNK), bool))
    w_causal = jnp.where(causal, w_s, 0.0)
    s = jnp.einsum('gts,bnsgd->bntgd', w_causal, vb) + b_s.T[:, :, None]
    s = s.reshape(B, n_chunks * A_CHUNK, A_INNER)[:, :T]
    return (u * s) @ w_out, v


def gated_delta_chunked(q, k, v, g, beta, s0):
    B, T, H, _ = q.shape
    DV = v.shape[-1]
    C = min(GDN_CHUNK, T)
    n = -(-T // C)
    pad = n * C - T

    def blocks(a):
        a = jnp.pad(a, [(0, 0), (0, pad)] + [(0, 0)] * (a.ndim - 2))
        a = a.reshape((B, n, C) + a.shape[2:])
        return jnp.moveaxis(a, (1, 3), (0, 2)).astype(jnp.float32)

    qc, kc, vc, gc, bc = blocks(q), blocks(k), blocks(v), blocks(g), blocks(beta)
    gcum = jnp.cumsum(gc, axis=-1)
    incl = jnp.tril(jnp.ones((C, C), bool))
    strict = jnp.tril(jnp.ones((C, C), bool), -1)
    diff = gcum[..., :, None] - gcum[..., None, :]
    decay = jnp.where(incl, jnp.exp(jnp.where(incl, diff, 0.0)), 0.0)
    kbeta = kc * bc[..., None]
    a_mat = jnp.where(strict, jnp.einsum('nbhid,nbhjd->nbhij', kbeta, kc) * decay, 0.0)
    eye = jnp.eye(C, dtype=jnp.float32)
    t_inv = lax.linalg.triangular_solve(a_mat + eye, jnp.broadcast_to(eye, a_mat.shape),
                                        left_side=True, lower=True, unit_diagonal=True)
    u = t_inv @ (vc * bc[..., None])
    w = t_inv @ (kbeta * jnp.exp(gcum)[..., None])
    qk = jnp.where(incl, jnp.einsum('nbhid,nbhjd->nbhij', qc, kc) * decay, 0.0)

    def step(s, inp):
        q_i, k_i, u_i, w_i, g_i, qk_i = inp
        v_new = u_i - w_i @ s
        o_i = (q_i * jnp.exp(g_i)[..., None]) @ s + qk_i @ v_new
        g_last = g_i[..., -1:]
        s = s * jnp.exp(g_last)[..., None] + jnp.einsum(
            'bhcd,bhce->bhde', k_i * jnp.exp(g_last - g_i)[..., None], v_new)
        return s, o_i

    s_final, o = lax.scan(step, s0.astype(jnp.float32), (qc, kc, u, w, gcum, qk))
    o = jnp.moveaxis(o, (0, 2), (1, 3)).reshape(B, n * C, H, DV)[:, :T]
    return o, s_final.astype(s0.dtype)


def gated_deltanet_mixer(x, conv_buf, s0, w_in, conv_w, a_log, dt_bias, norm_g, w_out):
    B, T, _ = x.shape
    proj = x @ w_in
    qkv = proj[..., :GDN_QKV]
    z = proj[..., GDN_QKV:GDN_QKV + GDN_WV]
    b_logit = proj[..., GDN_QKV + GDN_WV:GDN_QKV + GDN_WV + GDN_H]
    a_logit = proj[..., GDN_QKV + GDN_WV + GDN_H:]
    xc = jnp.concatenate([conv_buf.astype(qkv.dtype), qkv], axis=1)
    conv = xc[:, 0:T] * conv_w[0]
    for j in range(1, GDN_CONV):
        conv = conv + xc[:, j:j + T] * conv_w[j]
    qkv = jax.nn.silu(conv)
    new_buf = xc[:, -(GDN_CONV - 1):]
    q = l2_normalize(qkv[..., :GDN_WK].reshape(B, T, GDN_H, GDN_DK)) * (GDN_DK ** -0.5)
    k = l2_normalize(qkv[..., GDN_WK:2 * GDN_WK].reshape(B, T, GDN_H, GDN_DK))
    v = qkv[..., 2 * GDN_WK:].reshape(B, T, GDN_H, GDN_DV)
    beta = jax.nn.sigmoid(b_logit)
    g = -jnp.exp(a_log) * jax.nn.softplus(a_logit + dt_bias)
    o, s_new = gated_delta_chunked(q, k, v, g, beta, s0)
    o = rms_norm(o, norm_g) * jax.nn.silu(z.reshape(B, T, GDN_H, GDN_DV)).astype(jnp.float32)
    return o.astype(x.dtype).reshape(B, T, GDN_WV) @ w_out, new_buf, s_new


def nsa_compress(rows, pe, w1, w2):
    B, Tk, G, dh = rows.shape
    n_cmp = Tk // NSA_L_CMP
    blk = rows[:, :n_cmp * NSA_L_CMP].reshape(B, n_cmp, NSA_L_CMP, G, dh) + pe[:, None, :]
    flat = jnp.swapaxes(blk, 2, 3).reshape(B, n_cmp, G, NSA_L_CMP * dh)
    return jax.nn.gelu(flat @ w1) @ w2


def nsa_compressed_attn(q, q_pos, k_cmp, v_cmp, tab):
    B, T, H, dh = q.shape
    n_cmp = k_cmp.shape[1]
    qg = q.reshape(B, T, NSA_G, NSA_HG, dh)
    s = jnp.einsum('btghd,bngd->bghtn', qg, k_cmp).astype(jnp.float32) * (dh ** -0.5)
    blk_end = jnp.arange(n_cmp) * NSA_L_CMP + NSA_L_CMP - 1
    dist = q_pos[:, None] - blk_end[None, :]
    bias = t5_bias(tab, dist).reshape(T, n_cmp, NSA_G, NSA_HG).transpose(2, 3, 0, 1)
    p = masked_softmax(s + bias, dist >= 0)
    o = jnp.einsum('bghtn,bngd->btghd', p.astype(v_cmp.dtype), v_cmp).reshape(B, T, H, dh)
    return o, p


def nsa_select_blocks(p_cmp, q_pos, n_keys):
    n_slc = -(-n_keys // NSA_L_SLC)
    ps = p_cmp.sum(axis=2)
    ps = jnp.pad(ps, ((0, 0), (0, 0), (0, 0), (0, n_slc * NSA_CMP_PER_SLC - ps.shape[-1])))
    ps = ps.reshape(ps.shape[:3] + (n_slc, NSA_CMP_PER_SLC)).sum(-1)
    j = jnp.arange(n_slc)[None, :]
    jq = (q_pos // NSA_L_SLC)[:, None]
    forced = (j == 0) | (j == jq) | (j == jq - 1)
    score = jnp.where(forced, NSA_FORCE, jnp.where(j > jq, -1.0, ps))
    _, idx = lax.top_k(score, min(NSA_N_SEL, n_slc))
    return idx


def nsa_selected_attn(q, q_pos, k_slc, v_slc, blk_idx, tab):
    B, T, H, dh = q.shape
    Tk = k_slc.shape[1]
    n_slc = -(-Tk // NSA_L_SLC)
    n_sel = blk_idx.shape[-1]

    def to_blocks(a):
        a = jnp.pad(a, ((0, 0), (0, n_slc * NSA_L_SLC - Tk), (0, 0), (0, 0)))
        return a.reshape(B, n_slc, NSA_L_SLC, NSA_G, dh).transpose(0, 3, 1, 2, 4)

    kb, vb = to_blocks(k_slc), to_blocks(v_slc)
    N = B * T
    n_steps = -(-N // NSA_SEL_ROWS)
    pad = n_steps * NSA_SEL_ROWS - N

    def rows(a):
        a = jnp.pad(a, [(0, pad)] + [(0, 0)] * (a.ndim - 1))
        return a.reshape((n_steps, NSA_SEL_ROWS) + a.shape[1:])

    rq = rows(q.reshape(N, NSA_G, NSA_HG, dh))
    rb = rows(jnp.repeat(jnp.arange(B, dtype=jnp.int32), T))
    rt = rows(jnp.tile(q_pos, B))
    ri = rows(jnp.swapaxes(blk_idx, 1, 2).reshape(N, NSA_G, n_sel))
    g_ids = jnp.arange(NSA_G)[None, :, None]
    tab_g = tab.astype(jnp.float32).reshape(NUM_BUCKETS, NSA_G, NSA_HG)
    offs = jnp.arange(NSA_L_SLC)

    def step(args):
        q_r, b_r, t_r, i_r = args
        k_g = kb[b_r[:, None, None], g_ids, i_r]
        v_g = vb[b_r[:, None, None], g_ids, i_r]
        k_pos = i_r[..., None] * NSA_L_SLC + offs
        dist = t_r[:, None, None, None] - k_pos
        bias = jnp.moveaxis(tab_g[t5_bucket(dist), jnp.arange(NSA_G)[None, :, None, None]], -1, 2)
        s = jnp.einsum('rghd,rgnld->rghnl', q_r, k_g).astype(jnp.float32) * (dh ** -0.5) + bias
        p = masked_softmax(s, (dist >= 0)[:, :, None], axis=(-2, -1))
        return jnp.einsum('rghnl,rgnld->rghd', p.astype(v_g.dtype), v_g)

    o = lax.map(step, (rq, rb, rt, ri))
    return o.reshape(n_steps * NSA_SEL_ROWS, H, dh)[:N].reshape(B, T, H, dh)


def nsa_window_attn(q, q_pos0, k_win, v_win, tab):
    B, T, H, dh = q.shape
    Wp = k_win.shape[1] - T
    QB = min(NSA_WIN_QBLK, T)
    nqb = -(-T // QB)
    front = NSA_WINDOW - Wp
    back = nqb * QB - T
    K = NSA_WINDOW + QB
    kidx = jnp.arange(nqb)[:, None] * QB + jnp.arange(K)[None, :]
    kb = jnp.pad(k_win, ((0, 0), (front, back), (0, 0), (0, 0)))[:, kidx]
    vb = jnp.pad(v_win, ((0, 0), (front, back), (0, 0), (0, 0)))[:, kidx]
    qb = jnp.pad(q, ((0, 0), (0, back), (0, 0), (0, 0))).reshape(B, nqb, QB, NSA_G, NSA_HG, dh)
    qpos = q_pos0 + jnp.arange(nqb * QB).reshape(nqb, QB)
    kpos = q_pos0 - NSA_WINDOW + kidx
    dist = qpos[:, :, None] - kpos[:, None, :]
    mask = (dist >= 0) & (dist < NSA_WINDOW) & (kpos[:, None, :] >= 0)
    bias = t5_bias(tab, dist).reshape(nqb, QB, K, NSA_G, NSA_HG).transpose(0, 3, 4, 1, 2)
    s = jnp.einsum('bjqghd,bjkgd->bjghqk', qb, kb).astype(jnp.float32) * (dh ** -0.5) + bias
    p = masked_softmax(s, mask[:, None, None])
    o = jnp.einsum('bjghqk,bjkgd->bjqghd', p.astype(vb.dtype), vb)
    return o.reshape(B, nqb * QB, H, dh)[:, :T]


def nsa_mixer(x, q_pos0, kv_past, win_past, w_in, gate_b, cmp_pe, cmp_w1, cmp_w2, tab, w_out):
    B, T, _ = x.shape
    proj = x @ w_in
    q = proj[..., :NSA_Q].reshape(B, T, NSA_H, NSA_DH)
    kv_new = proj[..., NSA_Q:NSA_Q + 4 * NSA_KV].reshape(B, T, 4, NSA_G, NSA_DH)
    win_new = proj[..., NSA_Q + 4 * NSA_KV:NSA_Q + 6 * NSA_KV].reshape(B, T, 2, NSA_G, NSA_DH)
    gates = jax.nn.sigmoid(proj[..., NSA_Q + 6 * NSA_KV:] + gate_b).reshape(B, T, 3, NSA_H)
    kv = jnp.concatenate([kv_past.astype(x.dtype), kv_new], axis=1)
    win = jnp.concatenate([win_past.astype(x.dtype), win_new], axis=1)
    q_pos = q_pos0 + jnp.arange(T)
    k_cmp = nsa_compress(kv[:, :, 0], cmp_pe[0], cmp_w1[0], cmp_w2[0])
    v_cmp = nsa_compress(kv[:, :, 1], cmp_pe[1], cmp_w1[1], cmp_w2[1])
    o_cmp, p_cmp = nsa_compressed_attn(q, q_pos, k_cmp, v_cmp, tab)
    blk_idx = nsa_select_blocks(p_cmp, q_pos, kv.shape[1])
    o_slc = nsa_selected_attn(q, q_pos, kv[:, :, 2], kv[:, :, 3], blk_idx, tab)
    o_win = nsa_window_attn(q, q_pos0, win[:, :, 0], win[:, :, 1], tab)
    o = (gates[:, :, 0, :, None] * o_cmp + gates[:, :, 1, :, None] * o_slc
         + gates[:, :, 2, :, None] * o_win)
    return o.reshape(B, T, NSA_Q) @ w_out, kv_new, win_new


def setup_inputs(seed: int = 0) -> dict:
    key = jax.random.key(seed)
    keys = jax.random.split(key, 48)
    counter = [0]

    def nxt():
        counter[0] += 1
        return keys[counter[0] - 1]

    def rnd(shape, scale):
        return jax.random.normal(nxt(), shape, jnp.float32) * scale

    n_pages = PAST_LEN // PAGE_SIZE
    n_used = DEC_BATCH * n_pages
    n_phys = n_used + max(1, n_used // 4)
    wbuf = min(NSA_WINDOW, PAST_LEN)
    f32 = jnp.float32
    x_prompt = rnd((BATCH, SEQ, D_MODEL), 1.0)
    x_sample = rnd((DEC_BATCH, DEC_SEQ, D_MODEL), 1.0)
    state_gdn_s = rnd((N_B, DEC_BATCH, GDN_H, GDN_DK, GDN_DV), 0.5)
    state_gdn_conv = rnd((N_B, DEC_BATCH, GDN_CONV - 1, GDN_QKV), 1.0)
    cache_nsa_kv = rnd((N_C, n_phys, PAGE_SIZE, 4, NSA_G, NSA_DH), 1.0)
    cache_nsa_win = rnd((N_C, DEC_BATCH, wbuf, 2, NSA_G, NSA_DH), 1.0)
    page_table = jax.random.permutation(nxt(), n_phys)[:n_used].reshape(DEC_BATCH, n_pages).astype(jnp.int32)
    p_prompt = rnd((DEPTH, BATCH, SEQ, PLE_DIM), 1.0)
    p_sample = rnd((DEPTH, DEC_BATCH, DEC_SEQ, PLE_DIM), 1.0)
    ln_g = 1.0 + rnd((DEPTH, 3, D_MODEL), 0.05)
    ln_b = rnd((DEPTH, 3, D_MODEL), 0.02)
    ffn_w_up = rnd((DEPTH, 2, D_MODEL, 2 * D_FF), D_MODEL ** -0.5)
    ffn_w_down = rnd((DEPTH, 2, D_FF, D_MODEL), D_FF ** -0.5 * DEEPNORM_BETA)
    ple_w_gate = rnd((DEPTH, D_MODEL, D_MODEL), D_MODEL ** -0.5)
    ple_w_proj = rnd((DEPTH, PLE_DIM, D_MODEL), PLE_DIM ** -0.5)
    a_w_in = rnd((N_A, D_MODEL, 2 * A_INNER), D_MODEL ** -0.5)
    a_ln_g = 1.0 + rnd((N_A, A_INNER), 0.05)
    a_ln_b = rnd((N_A, A_INNER), 0.02)
    a_w_s = rnd((N_A, A_GROUPS, A_CHUNK, A_CHUNK), 0.5 * A_CHUNK ** -0.5)
    a_b_s = 1.0 + rnd((N_A, A_GROUPS, A_CHUNK), 0.1)
    a_w_out = rnd((N_A, A_INNER, D_MODEL), A_INNER ** -0.5 * DEEPNORM_BETA)
    gdn_w_in = rnd((N_B, D_MODEL, GDN_PROJ), D_MODEL ** -0.5)
    gdn_conv_w = rnd((N_B, GDN_CONV, GDN_QKV), GDN_CONV ** -0.5)
    gdn_a_log = jnp.log(jax.random.uniform(nxt(), (N_B, GDN_H), f32, 1.0, 16.0))
    dt = jnp.exp(jax.random.uniform(nxt(), (N_B, GDN_H), f32, math.log(1e-3), math.log(0.1)))
    gdn_dt_bias = dt + jnp.log(-jnp.expm1(-dt))
    gdn_norm_g = 1.0 + rnd((N_B, GDN_DV), 0.05)
    gdn_w_out = rnd((N_B, GDN_WV, D_MODEL), GDN_WV ** -0.5 * DEEPNORM_BETA)
    nsa_w_in = rnd((N_C, D_MODEL, NSA_PROJ), D_MODEL ** -0.5)
    nsa_gate_b = rnd((N_C, 3 * NSA_H), 0.01)
    nsa_cmp_pe = rnd((N_C, 2, NSA_L_CMP, NSA_DH), 0.1)
    nsa_cmp_w1 = rnd((N_C, 2, NSA_L_CMP * NSA_DH, NSA_CMP_HID), (NSA_L_CMP * NSA_DH) ** -0.5)
    nsa_cmp_w2 = rnd((N_C, 2, NSA_CMP_HID, NSA_DH), NSA_CMP_HID ** -0.5)
    nsa_w_out = rnd((N_C, NSA_Q, D_MODEL), NSA_Q ** -0.5 * DEEPNORM_BETA)
    t5_bias_table = rnd((NUM_BUCKETS, NSA_H), 0.5)
    return {'x_prompt': x_prompt, 'x_sample': x_sample, 'state_gdn_s': state_gdn_s,
            'state_gdn_conv': state_gdn_conv, 'cache_nsa_kv': cache_nsa_kv,
            'cache_nsa_win': cache_nsa_win, 'page_table': page_table,
            'p_prompt': p_prompt, 'p_sample': p_sample, 'ln_g': ln_g, 'ln_b': ln_b,
            'ffn_w_up': ffn_w_up, 'ffn_w_down': ffn_w_down, 'ple_w_gate': ple_w_gate,
            'ple_w_proj': ple_w_proj, 'a_w_in': a_w_in, 'a_ln_g': a_ln_g, 'a_ln_b': a_ln_b,
            'a_w_s': a_w_s, 'a_b_s': a_b_s, 'a_w_out': a_w_out, 'gdn_w_in': gdn_w_in,
            'gdn_conv_w': gdn_conv_w, 'gdn_a_log': gdn_a_log, 'gdn_dt_bias': gdn_dt_bias,
            'gdn_norm_g': gdn_norm_g, 'gdn_w_out': gdn_w_out, 'nsa_w_in': nsa_w_in,
            'nsa_gate_b': nsa_gate_b, 'nsa_cmp_pe': nsa_cmp_pe, 'nsa_cmp_w1': nsa_cmp_w1,
            'nsa_cmp_w2': nsa_cmp_w2, 'nsa_w_out': nsa_w_out, 't5_bias_table': t5_bias_table}


def reference(x_prompt, x_sample, state_gdn_s, state_gdn_conv, cache_nsa_kv, cache_nsa_win,
              page_table, p_prompt, p_sample, ln_g, ln_b, ffn_w_up, ffn_w_down, ple_w_gate,
              ple_w_proj, a_w_in, a_ln_g, a_ln_b, a_w_s, a_b_s, a_w_out, gdn_w_in, gdn_conv_w,
              gdn_a_log, gdn_dt_bias, gdn_norm_g, gdn_w_out, nsa_w_in, nsa_gate_b, nsa_cmp_pe,
              nsa_cmp_w1, nsa_cmp_w2, nsa_w_out, t5_bias_table):
    bp, tp, _ = x_prompt.shape
    bs = x_sample.shape[0]
    n_pages = page_table.shape[1]
    past_len = n_pages * cache_nsa_kv.shape[2]
    xp, xs = x_prompt, x_sample
    a_v_s, gdn_s_p, gdn_c_p, gdn_s_s, gdn_c_s = [], [], [], [], []
    kv_p, win_p, kv_s, win_s = [], [], [], []
    ia = ib = ic = 0
    for i in range(DEPTH):
        xp = post_norm(xp, 0.5 * swiglu(xp, ffn_w_up[i, 0], ffn_w_down[i, 0]), ln_g[i, 0], ln_b[i, 0])
        xs = post_norm(xs, 0.5 * swiglu(xs, ffn_w_up[i, 0], ffn_w_down[i, 0]), ln_g[i, 0], ln_b[i, 0])
        kind = i % N_MIXERS
        if kind == 0:
            yp, _ = chunk_mlp_mixer(xp, a_w_in[ia], a_ln_g[ia], a_ln_b[ia], a_w_s[ia], a_b_s[ia], a_w_out[ia])
            ys, v_rows = chunk_mlp_mixer(xs, a_w_in[ia], a_ln_g[ia], a_ln_b[ia], a_w_s[ia], a_b_s[ia], a_w_out[ia])
            a_v_s.append(v_rows)
            ia += 1
        elif kind == 1:
            zero_buf = jnp.zeros((bp, GDN_CONV - 1, GDN_QKV), xp.dtype)
            zero_s = jnp.zeros((bp, GDN_H, GDN_DK, GDN_DV), xp.dtype)
            yp, buf_p, s_p = gated_deltanet_mixer(xp, zero_buf, zero_s, gdn_w_in[ib], gdn_conv_w[ib],
                                                  gdn_a_log[ib], gdn_dt_bias[ib], gdn_norm_g[ib], gdn_w_out[ib])
            ys, buf_s, s_s = gated_deltanet_mixer(xs, state_gdn_conv[ib], state_gdn_s[ib], gdn_w_in[ib],
                                                  gdn_conv_w[ib], gdn_a_log[ib], gdn_dt_bias[ib],
                                                  gdn_norm_g[ib], gdn_w_out[ib])
            gdn_s_p.append(s_p)
            gdn_c_p.append(buf_p)
            gdn_s_s.append(s_s)
            gdn_c_s.append(buf_s)
            ib += 1
        else:
            empty_kv = jnp.zeros((bp, 0, 4, NSA_G, NSA_DH), xp.dtype)
            empty_win = jnp.zeros((bp, 0, 2, NSA_G, NSA_DH), xp.dtype)
            yp, kvn_p, winn_p = nsa_mixer(xp, 0, empty_kv, empty_win, nsa_w_in[ic], nsa_gate_b[ic],
                                          nsa_cmp_pe[ic], nsa_cmp_w1[ic], nsa_cmp_w2[ic], t5_bias_table,
                                          nsa_w_out[ic])
            kv_past = cache_nsa_kv[ic][page_table].reshape(bs, past_len, 4, NSA_G, NSA_DH)
            ys, kvn_s, winn_s = nsa_mixer(xs, past_len, kv_past, cache_nsa_win[ic], nsa_w_in[ic],
                                          nsa_gate_b[ic], nsa_cmp_pe[ic], nsa_cmp_w1[ic], nsa_cmp_w2[ic],
                                          t5_bias_table, nsa_w_out[ic])
            kv_p.append(kvn_p)
            win_p.append(winn_p[:, tp - min(NSA_WINDOW, tp):])
            kv_s.append(kvn_s)
            win_s.append(winn_s)
            ic += 1
        xp = post_norm(xp, yp, ln_g[i, 1], ln_b[i, 1])
        xs = post_norm(xs, ys, ln_g[i, 1], ln_b[i, 1])
        xp = post_norm(xp, 0.5 * swiglu(xp, ffn_w_up[i, 1], ffn_w_down[i, 1]), ln_g[i, 2], ln_b[i, 2])
        xs = post_norm(xs, 0.5 * swiglu(xs, ffn_w_up[i, 1], ffn_w_down[i, 1]), ln_g[i, 2], ln_b[i, 2])
        xp = xp + jax.nn.sigmoid(xp @ ple_w_gate[i]) * (p_prompt[i] @ ple_w_proj[i])
        xs = xs + jax.nn.sigmoid(xs @ ple_w_gate[i]) * (p_sample[i] @ ple_w_proj[i])
    return (xp, xs, jnp.stack(a_v_s), jnp.stack(gdn_s_p), jnp.stack(gdn_c_p), jnp.stack(gdn_s_s),
            jnp.stack(gdn_c_s), jnp.stack(kv_p), jnp.stack(win_p), jnp.stack(kv_s), jnp.stack(win_s))
```

```python
import functools
import math

import jax
import jax.numpy as jnp
from jax import lax
from jax.experimental import pallas as pl
from jax.experimental.pallas import tpu as pltpu

F32 = jnp.float32
BF16 = jnp.bfloat16
HIGHEST = lax.Precision.HIGHEST

D_MODEL = 1024
DEPTH = 4
ALPHA = (2 * DEPTH) ** 0.25
LN_EPS = 1e-5
NORM_EPS = 1e-6
D_FF = 2816
A_CHUNK = 128
A_INNER = 2 * D_MODEL
A_GROUPS = 16
GDN_H = 8
GDN_D = 128
GDN_QKV = 3 * D_MODEL
GDN_CONV = 4
GDN_CHUNK = 64
NSA_H = 16
NSA_DH = 64
NSA_G = 4
NSA_HG = 4
NSA_L_CMP = 32
NSA_L_SLC = 64
NSA_N_SEL = 16
NSA_WINDOW = 512
NSA_FORCE = 100.0
NEG_INF = -1e30
NUM_BUCKETS = 32
MAX_DISTANCE = 128

ROW_TILE = 256
ATT_TILE = 128
VMEM_LIMIT = 56 << 20


def _cparams(*sem):
    return pltpu.CompilerParams(dimension_semantics=sem, vmem_limit_bytes=VMEM_LIMIT)


def _resident(shape):
    nd = len(shape)
    return pl.BlockSpec(shape, lambda *_: (0,) * nd, pipeline_mode=pl.Buffered(1))


def _rows(tm, width):
    return pl.BlockSpec((tm, width), lambda i: (i, 0))


def _bdot(a, b):
    return jnp.dot(a.astype(BF16), b.astype(BF16), preferred_element_type=F32)


def _bdot_nt(a, b):
    return lax.dot_general(a.astype(BF16), b.astype(BF16), (((1,), (1,)), ((), ())),
                           preferred_element_type=F32)


def _hdot(a, b):
    return jnp.dot(a, b, precision=HIGHEST, preferred_element_type=F32)


def _sigmoid(x):
    return 1.0 / (1.0 + jnp.exp(-x))


def _silu(x):
    return x * _sigmoid(x)


def _gelu(x):
    return 0.5 * x * (1.0 + jnp.tanh(math.sqrt(2.0 / math.pi) * (x + 0.044715 * (x * x * x))))


def _softplus(x):
    return jnp.maximum(x, 0.0) + jnp.log(1.0 + jnp.exp(-jnp.abs(x)))


def _ln(y, g, b):
    mu = jnp.mean(y, axis=-1, keepdims=True)
    d = y - mu
    var = jnp.mean(d * d, axis=-1, keepdims=True)
    return d * lax.rsqrt(var + LN_EPS) * g + b


def _ffn_body(*refs, pre_norm, ple):
    it = iter(refs)
    x_ref = next(it)
    if pre_norm:
        y_ref, g1, b1 = next(it), next(it), next(it)
    wu, wd, g2, b2 = next(it), next(it), next(it), next(it)
    if ple:
        p_ref, wg, wp = next(it), next(it), next(it)
    o_ref = next(it)
    x = x_ref[...]
    if pre_norm:
        x = _ln(ALPHA * x + y_ref[...], g1[...], b1[...])
    h = _bdot(x, wu[...])
    act = _silu(h[:, :D_FF]) * h[:, D_FF:]
    x = _ln(ALPHA * x + 0.5 * _bdot(act, wd[...]), g2[...], b2[...])
    if ple:
        x = x + _sigmoid(_bdot(x, wg[...])) * _bdot(p_ref[...], wp[...])
    o_ref[...] = x


def _ffn(x, wu, wd, g2, b2, pre=None, ple=None):
    n = x.shape[0]
    tm = ROW_TILE
    args, specs = [x], [_rows(tm, D_MODEL)]
    if pre is not None:
        y, g1, b1 = pre
        args += [y, g1, b1]
        specs += [_rows(tm, D_MODEL), _resident((1, D_MODEL)), _resident((1, D_MODEL))]
    args += [wu, wd, g2, b2]
    specs += [_resident(wu.shape), _resident(wd.shape), _resident((1, D_MODEL)), _resident((1, D_MODEL))]
    if ple is not None:
        p, wg, wp = ple
        args += [p, wg, wp]
        specs += [_rows(tm, p.shape[1]), _resident(wg.shape), _resident(wp.shape)]
    return pl.pallas_call(
        functools.partial(_ffn_body, pre_norm=pre is not None, ple=ple is not None),
        grid=(n // tm,), in_specs=specs, out_specs=_rows(tm, D_MODEL),
        out_shape=jax.ShapeDtypeStruct((n, D_MODEL), F32),
        compiler_params=_cparams("parallel"), name="ffn")(*args)


def _mixa_body(x_ref, win, lg, lb, ws, bs, wout, y_ref, *rest, decode):
    if decode:
        v_ref, s_ref = rest
    else:
        (s_ref,) = rest
    tm = x_ref.shape[0]
    h = _gelu(_bdot(x_ref[...], win[...]))
    u = h[:, :A_INNER]
    v = _ln(h[:, A_INNER:], lg[...], lb[...])
    if decode:
        v_ref[...] = v
        nb = tm // 4
        for t in range(4):
            acc = bs[t:t + 1, :]
            for s in range(t + 1):
                acc = acc + ws[4 * t + s:4 * t + s + 1, :] * v[s * nb:(s + 1) * nb, :]
            s_ref[t * nb:(t + 1) * nb, :] = acc
    else:
        for c in range(tm // A_CHUNK):
            r = slice(c * A_CHUNK, (c + 1) * A_CHUNK)
            for g in range(A_GROUPS):
                l = slice(g * 128, (g + 1) * 128)
                s_ref[r, l] = jnp.dot(ws[g], v[r, l].astype(BF16), preferred_element_type=F32) + bs[:, l]
    y_ref[...] = _bdot(u * s_ref[...], wout[...])


def _mix_a(x, win, lg, lb, ws, bs, wout, decode):
    n = x.shape[0]
    tm = n if decode else ROW_TILE
    out_shape = [jax.ShapeDtypeStruct((n, D_MODEL), F32)]
    out_specs = [_rows(tm, D_MODEL)]
    if decode:
        out_shape.append(jax.ShapeDtypeStruct((n, A_INNER), F32))
        out_specs.append(_rows(tm, A_INNER))
    res = pl.pallas_call(
        functools.partial(_mixa_body, decode=decode),
        grid=(n // tm,),
        in_specs=[_rows(tm, D_MODEL), _resident(win.shape), _resident((1, A_INNER)), _resident((1, A_INNER)),
                  _resident(ws.shape), _resident(bs.shape), _resident(wout.shape)],
        out_specs=out_specs, out_shape=out_shape,
        scratch_shapes=[pltpu.VMEM((tm, A_INNER), F32)],
        compiler_params=_cparams("parallel"), name="mix_a_dec" if decode else "mix_a")(x, win, lg, lb, ws, bs, wout)
    return res


def _gdn_post(act, x, wz, wba, arow, dtrow, q_o, k_o, v_o, z_o, ba_o):
    for h in range(GDN_H):
        l = slice(h * GDN_D, (h + 1) * GDN_D)
        qh = act[:, l]
        q_o[:, l] = qh * lax.rsqrt(jnp.sum(qh * qh, axis=-1, keepdims=True) + NORM_EPS) * (GDN_D ** -0.5)
        kh = act[:, D_MODEL + h * GDN_D:D_MODEL + (h + 1) * GDN_D]
        k_o[:, l] = kh * lax.rsqrt(jnp.sum(kh * kh, axis=-1, keepdims=True) + NORM_EPS)
    v_o[...] = act[:, 2 * D_MODEL:]
    z_o[...] = _bdot(x, wz[...])
    lg = _bdot(x, wba[...])
    lane = lax.broadcasted_iota(jnp.int32, lg.shape, 1)
    ba_o[...] = jnp.where(lane < GDN_H, _sigmoid(lg), arow[...] * _softplus(lg + dtrow[...]))


def _gdn_pre_body(x_ref, xh_ref, wqkv, wz, wba, cw, arow, dtrow, q_o, k_o, v_o, z_o, ba_o, tail_o, ext,
                  *, tiles_per_seq):
    tm = x_ref.shape[0]
    x = x_ref[...]
    pre = _bdot(x, wqkv[...])
    halo = _bdot(xh_ref[...], wqkv[...])
    first = pl.program_id(0) % tiles_per_seq == 0
    ext[0:8, :] = jnp.where(first, 0.0, halo)
    ext[8:8 + tm, :] = pre
    tail_o[0] = pre[tm - 8:, :]
    conv = ext[pl.ds(5, tm), :] * cw[0:1, :]
    for j in range(1, GDN_CONV):
        conv = conv + ext[pl.ds(5 + j, tm), :] * cw[j:j + 1, :]
    _gdn_post(_silu(conv), x, wz, wba, arow, dtrow, q_o, k_o, v_o, z_o, ba_o)


def _gdn_pre_dec_body(x_ref, buf_ref, wqkv, wz, wba, cw, arow, dtrow, q_o, k_o, v_o, z_o, ba_o, nbuf_o, ext):
    n = x_ref.shape[0]
    nb = n // 4
    x = x_ref[...]
    ext[0:3 * nb, :] = buf_ref[...]
    ext[3 * nb:3 * nb + n, :] = _bdot(x, wqkv[...])
    nbuf_o[...] = ext[4 * nb:7 * nb, :]
    conv = ext[0:n, :] * cw[0:1, :]
    for j in range(1, GDN_CONV):
        conv = conv + ext[j * nb:j * nb + n, :] * cw[j:j + 1, :]
    _gdn_post(_silu(conv), x, wz, wba, arow, dtrow, q_o, k_o, v_o, z_o, ba_o)


def _gdn_pre(x, wqkv, wz, wba, cw, arow, dtrow, seq_len=None, conv_buf=None):
    n = x.shape[0]
    w_specs = [_resident(wqkv.shape), _resident(wz.shape), _resident(wba.shape), _resident(cw.shape),
               _resident((1, 128)), _resident((1, 128))]
    outs = [jax.ShapeDtypeStruct((n, D_MODEL), F32)] * 4 + [jax.ShapeDtypeStruct((n, 128), F32)]
    if conv_buf is None:
        tm = ROW_TILE
        tps = seq_len // tm
        nseq = n // seq_len
        return pl.pallas_call(
            functools.partial(_gdn_pre_body, tiles_per_seq=tps), grid=(n // tm,),
            in_specs=[_rows(tm, D_MODEL),
                      pl.BlockSpec((8, D_MODEL), lambda i: (jnp.maximum(i * (tm // 8) - 1, 0), 0))] + w_specs,
            out_specs=[_rows(tm, D_MODEL)] * 4 + [_rows(tm, 128),
                                                   pl.BlockSpec((1, 8, GDN_QKV), lambda i: (i // tps, 0, 0))],
            out_shape=outs + [jax.ShapeDtypeStruct((nseq, 8, GDN_QKV), F32)],
            scratch_shapes=[pltpu.VMEM((tm + 8, GDN_QKV), F32)],
            compiler_params=_cparams("arbitrary"), name="gdn_pre")(x, x, wqkv, wz, wba, cw, arow, dtrow)
    nb = n // 4
    return pl.pallas_call(
        _gdn_pre_dec_body, grid=(1,),
        in_specs=[_rows(n, D_MODEL), _rows(3 * nb, GDN_QKV)] + w_specs,
        out_specs=[_rows(n, D_MODEL)] * 4 + [_rows(n, 128), _rows(3 * nb, GDN_QKV)],
        out_shape=outs + [jax.ShapeDtypeStruct((3 * nb, GDN_QKV), F32)],
        scratch_shapes=[pltpu.VMEM((7 * nb, GDN_QKV), F32)],
        compiler_params=_cparams("arbitrary"), name="gdn_pre_dec")(x, conv_buf, wqkv, wz, wba, cw, arow, dtrow)


def _gdn_scan_body(q_ref, k_ref, v_ref, ba_ref, s0_ref, o_ref, s_ref):
    c = q_ref.shape[0]

    @pl.when(pl.program_id(1) == 0)
    def _():
        s_ref[...] = s0_ref[...]

    ba = ba_ref[...]
    ri = lax.broadcasted_iota(jnp.int32, (c, c), 0)
    ci = lax.broadcasted_iota(jnp.int32, (c, c), 1)
    incl = ci <= ri
    strict = ci < ri
    eye = (ri == ci).astype(F32)
    gcum_all = _hdot(incl.astype(F32), ba)
    for h in range(GDN_H):
        l = slice(h * GDN_D, (h + 1) * GDN_D)
        qh, kh, vh = q_ref[:, l], k_ref[:, l], v_ref[:, l]
        beta = ba[:, h:h + 1]
        gc = gcum_all[:, GDN_H + h:GDN_H + h + 1]
        gc_row = jnp.sum(gc * eye, axis=0, keepdims=True)
        decay = jnp.where(incl, jnp.exp(jnp.where(incl, gc - gc_row, 0.0)), 0.0)
        kbeta = kh * beta
        a = jnp.where(strict, _bdot_nt(kbeta, kh) * decay, 0.0)
        inv = eye - a
        pw = a
        for _ in range(int(math.log2(c)) - 1):
            pw = _hdot(pw, pw)
            inv = inv + _hdot(inv, pw)
        u = _bdot(inv, vh * beta)
        w = _bdot(inv, kbeta * jnp.exp(gc))
        qk = jnp.where(incl, _bdot_nt(qh, kh) * decay, 0.0)
        s = s_ref[0, h]
        v_new = u - _bdot(w, s)
        o_ref[:, l] = _bdot(qh * jnp.exp(gc), s) + _bdot(qk, v_new)
        g_last = gc[c - 1:c, :]
        kd = kh * jnp.exp(g_last - gc)
        s_ref[0, h] = s * jnp.exp(g_last) + _bdot(kd.T, v_new)


def _gdn_scan(q, k, v, ba, s0, chunk):
    n = q.shape[0]
    nseq = s0.shape[0]
    nch = n // nseq // chunk
    row = lambda w: pl.BlockSpec((chunk, w), lambda b, c: (b * nch + c, 0))
    st = pl.BlockSpec((1, GDN_H, GDN_D, GDN_D), lambda b, c: (b, 0, 0, 0))
    return pl.pallas_call(
        _gdn_scan_body, grid=(nseq, nch),
        in_specs=[row(D_MODEL), row(D_MODEL), row(D_MODEL), row(128), st],
        out_specs=[row(D_MODEL), st],
        out_shape=[jax.ShapeDtypeStruct((n, D_MODEL), F32), jax.ShapeDtypeStruct(s0.shape, F32)],
        compiler_params=_cparams("parallel", "arbitrary"), name="gdn_scan")(q, k, v, ba, s0)


def _gdn_out_body(o_ref, z_ref, ng, wout, y_ref, t_ref):
    for h in range(GDN_H):
        l = slice(h * GDN_D, (h + 1) * GDN_D)
        oh = o_ref[:, l]
        t_ref[:, l] = oh * lax.rsqrt(jnp.mean(oh * oh, axis=-1, keepdims=True) + NORM_EPS)
    y_ref[...] = _bdot(t_ref[...] * ng[...] * _silu(z_ref[...]), wout[...])


def _gdn_out(o, z, ng, wout):
    n = o.shape[0]
    tm = ROW_TILE
    return pl.pallas_call(
        _gdn_out_body, grid=(n // tm,),
        in_specs=[_rows(tm, D_MODEL), _rows(tm, D_MODEL), _resident((1, D_MODEL)), _resident(wout.shape)],
        out_specs=_rows(tm, D_MODEL), out_shape=jax.ShapeDtypeStruct((n, D_MODEL), F32),
        scratch_shapes=[pltpu.VMEM((tm, D_MODEL), F32)],
        compiler_params=_cparams("parallel"), name="gdn_out")(o, z, ng, wout)


def _proj_body(x_ref, w_ref, *o_refs):
    y = _bdot(x_ref[...], w_ref[...])
    off = 0
    for o in o_refs:
        wdt = o.shape[1]
        o[...] = y[:, off:off + wdt]
        off += wdt


def _proj(x, w, widths):
    n = x.shape[0]
    tm = ROW_TILE
    return pl.pallas_call(
        _proj_body, grid=(n // tm,),
        in_specs=[_rows(tm, x.shape[1]), _resident(w.shape)],
        out_specs=[_rows(tm, wd) for wd in widths],
        out_shape=[jax.ShapeDtypeStruct((n, wd), F32) for wd in widths],
        compiler_params=_cparams("parallel"), name="proj")(x, w)


def _cmp_body(r_ref, pe_ref, w1, w2, o_ref):
    hid = _gelu(_bdot(r_ref[0] + pe_ref[0], w1[0]))
    o_ref[0] = _bdot(hid, w2[0])


def _compress(rows, pe, w1, w2):
    _, m, kdim = rows.shape
    tm = ROW_TILE
    return pl.pallas_call(
        _cmp_body, grid=(2, m // tm),
        in_specs=[pl.BlockSpec((1, tm, kdim), lambda c, i: (c, i, 0)),
                  pl.BlockSpec((1, 1, kdim), lambda c, i: (c, 0, 0)),
                  pl.BlockSpec((1,) + w1.shape[1:], lambda c, i: (c, 0, 0)),
                  pl.BlockSpec((1,) + w2.shape[1:], lambda c, i: (c, 0, 0))],
        out_specs=pl.BlockSpec((1, tm, NSA_DH), lambda c, i: (c, i, 0)),
        out_shape=jax.ShapeDtypeStruct((2, m, NSA_DH), F32),
        compiler_params=_cparams("parallel", "parallel"), name="nsa_compress")(rows, pe, w1, w2)


def _cmp_attn_body(q_ref, k_ref, vt_ref, b_ref, pair_ref, qpos_ref, o_ref, sel_ref, *, tq, n_slc):
    lanes = q_ref.shape[-1]
    bias = b_ref[0, 0]
    mask = bias > 0.5 * NEG_INF
    s = jnp.where(mask, _bdot(k_ref[0, 0], q_ref[0, 0, 0]) * (NSA_DH ** -0.5) + bias, NEG_INF)
    m = jnp.max(s, axis=0, keepdims=True)
    e = jnp.exp(s - m)
    p = jnp.where(mask, e / jnp.sum(e, axis=0, keepdims=True), 0.0)
    o_ref[0, 0, 0] = _bdot(vt_ref[0, 0], p)
    ps = p
    for hh in range(1, NSA_HG):
        ps = ps + pltpu.roll(p, hh * tq, axis=1)
    ps = _hdot(pair_ref[...], ps)
    w = tq if tq % 128 == 0 else lanes
    sc = ps[:, :w]
    n_p = sc.shape[0]
    j = lax.broadcasted_iota(jnp.int32, (n_p, w), 0)
    jq = qpos_ref[0, :, :w] // NSA_L_SLC
    forced = (j == 0) | (j == jq) | (j == jq - 1)
    sc = jnp.where(forced, NSA_FORCE, jnp.where(j > jq, -1.0, sc))
    sc = jnp.where(j >= n_slc, -2.0, sc)
    rank = jnp.zeros((n_p, w), F32)
    for i in range(n_slc):
        r = sc[i:i + 1, :]
        rank = rank + jnp.where((r > sc) | ((r == sc) & (j > i)), 1.0, 0.0)
    sel = jnp.where(rank < NSA_N_SEL, 1.0, 0.0)
    sel_ref[0, 0, 0] = sel if w == lanes else jnp.concatenate([sel] * NSA_HG, axis=1)


def _cmp_attn(q4, kc, vct, bias, pair, qpos, tq, n_slc):
    b, g, nq, _, lanes = q4.shape
    n_cmp = kc.shape[2]
    n_p = pair.shape[0]
    qs = pl.BlockSpec((1, 1, 1, NSA_DH, lanes), lambda b_, g_, i: (b_, g_, i, 0, 0))
    return pl.pallas_call(
        functools.partial(_cmp_attn_body, tq=tq, n_slc=n_slc), grid=(b, g, nq),
        in_specs=[qs,
                  pl.BlockSpec((1, 1, n_cmp, NSA_DH), lambda b_, g_, i: (b_, g_, 0, 0)),
                  pl.BlockSpec((1, 1, NSA_DH, n_cmp), lambda b_, g_, i: (b_, g_, 0, 0)),
                  pl.BlockSpec((1, 1, n_cmp, lanes), lambda b_, g_, i: (i, g_, 0, 0)),
                  pl.BlockSpec(pair.shape, lambda b_, g_, i: (0, 0)),
                  pl.BlockSpec((1, 1, lanes), lambda b_, g_, i: (i, 0, 0))],
        out_specs=[qs, pl.BlockSpec((1, 1, 1, n_p, lanes), lambda b_, g_, i: (b_, g_, i, 0, 0))],
        out_shape=[jax.ShapeDtypeStruct(q4.shape, F32), jax.ShapeDtypeStruct((b, g, nq, n_p, lanes), F32)],
        compiler_params=_cparams("parallel", "parallel", "parallel"), name="nsa_cmp_attn")(
            q4, kc, vct, bias, pair, qpos)


def _flash_step(q, k_tile, vt_tile, bias, sel_mask, m_s, l_s, acc_s):
    mask = bias > 0.5 * NEG_INF
    if sel_mask is not None:
        mask = mask & sel_mask
    s = jnp.where(mask, _bdot(k_tile, q) * (NSA_DH ** -0.5) + bias, NEG_INF)
    m_old = m_s[...]
    m_new = jnp.maximum(m_old, jnp.max(s, axis=0, keepdims=True))
    alpha = jnp.exp(m_old - m_new)
    p = jnp.where(mask, jnp.exp(s - m_new), 0.0)
    l_s[...] = alpha * l_s[...] + jnp.sum(p, axis=0, keepdims=True)
    acc_s[...] = alpha * acc_s[...] + _bdot(vt_tile, p)
    m_s[...] = m_new


def _flash_init(m_s, l_s, acc_s):
    m_s[...] = jnp.full(m_s.shape, NEG_INF, F32)
    l_s[...] = jnp.zeros(l_s.shape, F32)
    acc_s[...] = jnp.zeros(acc_s.shape, F32)


def _flash_out(l_s, acc_s):
    l = l_s[...]
    return acc_s[...] / jnp.where(l > 0.0, l, 1.0)


def _sw_attn_body(q_ref, ks_ref, vs_ref, sel_ref, et_ref, bs_ref, kw_ref, vw_ref, bw_ref, os_ref, ow_ref,
                  m_s, l_s, acc_s, *, decode):
    tk = ATT_TILE
    qt = pl.program_id(2)
    q = q_ref[0, 0, 0]
    sel = sel_ref[0, 0, 0].astype(BF16)
    n_sel_bias = bs_ref.shape[0]
    n_win = bw_ref.shape[0]

    _flash_init(m_s, l_s, acc_s)

    def sel_step(kt, carry):
        sel_mask = jnp.dot(et_ref[kt], sel, preferred_element_type=F32) > 0.5
        bidx = kt if decode else jnp.minimum(qt - kt, n_sel_bias - 1)
        _flash_step(q, ks_ref[0, 0, pl.ds(pl.multiple_of(kt * tk, tk), tk), :], vs_ref[0, 0, kt],
                    bs_ref[bidx, 0], sel_mask, m_s, l_s, acc_s)
        return carry

    lax.fori_loop(0, n_sel_bias if decode else qt + 1, sel_step, 0)
    os_ref[0, 0, 0] = _flash_out(l_s, acc_s)

    _flash_init(m_s, l_s, acc_s)
    for jw in range(n_win):
        kt = jw if decode else qt - jw

        def win_step(kt=kt, jw=jw):
            _flash_step(q, kw_ref[0, 0, pl.ds(pl.multiple_of(kt * tk, tk), tk), :], vw_ref[0, 0, kt],
                        bw_ref[jw, 0], None, m_s, l_s, acc_s)

        if decode:
            win_step()
        else:
            pl.when(kt >= 0)(win_step)
    ow_ref[0, 0, 0] = _flash_out(l_s, acc_s)


def _sw_attn(q4, ks, vst, sel, et, bsel, kw, vwt, bwin, decode):
    b, g, nq, _, lanes = q4.shape
    tk = ATT_TILE
    qs = pl.BlockSpec((1, 1, 1, NSA_DH, lanes), lambda b_, g_, i: (b_, g_, i, 0, 0))
    kspec = lambda a: pl.BlockSpec((1, 1) + a.shape[2:], lambda b_, g_, i: (b_, g_) + (0,) * (a.ndim - 2))
    bspec = lambda a: pl.BlockSpec((a.shape[0], 1, tk, lanes), lambda b_, g_, i: (0, g_, 0, 0))
    return pl.pallas_call(
        functools.partial(_sw_attn_body, decode=decode), grid=(b, g, nq),
        in_specs=[qs, kspec(ks), kspec(vst),
                  pl.BlockSpec((1, 1, 1) + sel.shape[3:], lambda b_, g_, i: (b_, g_, i, 0, 0)),
                  pl.BlockSpec(et.shape, lambda b_, g_, i: (0, 0, 0)), bspec(bsel),
                  kspec(kw), kspec(vwt), bspec(bwin)],
        out_specs=[qs, qs],
        out_shape=[jax.ShapeDtypeStruct(q4.shape, F32)] * 2,
        scratch_shapes=[pltpu.VMEM((1, lanes), F32), pltpu.VMEM((1, lanes), F32), pltpu.VMEM((NSA_DH, lanes), F32)],
        compiler_params=_cparams("parallel", "parallel", "arbitrary"),
        name="nsa_sw_attn_dec" if decode else "nsa_sw_attn")(q4, ks, vst, sel, et, bsel, kw, vwt, bwin)


def _nsa_out_body(x_ref, oc_ref, os_ref, ow_ref, wg, gb, wout, y_ref):
    gates = _sigmoid(_bdot_nt(wg[...], x_ref[...]) + gb[...])
    parts = []
    for h in range(NSA_H):
        r = slice(h * NSA_DH, (h + 1) * NSA_DH)
        parts.append(gates[h:h + 1, :] * oc_ref[r, :] + gates[NSA_H + h:NSA_H + h + 1, :] * os_ref[r, :]
                     + gates[2 * NSA_H + h:2 * NSA_H + h + 1, :] * ow_ref[r, :])
    o = jnp.concatenate(parts, axis=0)
    y_ref[...] = _bdot(o.T, wout[...])


def _nsa_out(x, oc, osl, ow, wg, gb, wout):
    n = x.shape[0]
    tm = ROW_TILE
    col = pl.BlockSpec((D_MODEL, tm), lambda i: (0, i))
    return pl.pallas_call(
        _nsa_out_body, grid=(n // tm,),
        in_specs=[_rows(tm, D_MODEL), col, col, col, _resident(wg.shape), _resident(gb.shape), _resident(wout.shape)],
        out_specs=_rows(tm, D_MODEL), out_shape=jax.ShapeDtypeStruct((n, D_MODEL), F32),
        compiler_params=_cparams("parallel"), name="nsa_out")(x, oc, osl, ow, wg, gb, wout)


def _t5_bucket(dist):
    n = jnp.maximum(dist, 0)
    max_exact = NUM_BUCKETS // 2
    nf = jnp.maximum(n, 1).astype(F32)
    large = max_exact + (jnp.log(nf / max_exact) / math.log(MAX_DISTANCE / max_exact)
                         * (NUM_BUCKETS - max_exact)).astype(jnp.int32)
    return jnp.where(n < max_exact, n, jnp.minimum(large, NUM_BUCKETS - 1))


def _bias_table(tab, dist, valid, tq):
    bias = jnp.where(valid[..., None], tab.astype(F32)[_t5_bucket(dist)], NEG_INF)
    bias = bias.reshape(bias.shape[:-1] + (NSA_G, NSA_HG))
    bias = jnp.moveaxis(bias, (-2, -1), (-4, -2))
    return bias.reshape(bias.shape[:-2] + (NSA_HG * tq,))


def _to_q4(q, nq, tq):
    b = q.shape[0]
    q = q.reshape(b, nq, tq, NSA_G, NSA_HG, NSA_DH)
    return q.transpose(0, 3, 1, 5, 4, 2).reshape(b, NSA_G, nq, NSA_DH, NSA_HG * tq)


def _from_q4(o4, tq):
    b, g, nq = o4.shape[:3]
    o = o4.reshape(b, g, nq, NSA_DH, NSA_HG, tq)
    return o.transpose(1, 4, 3, 0, 2, 5).reshape(NSA_H * NSA_DH, b * nq * tq)


def _kv_layout(k, v, nkt):
    b = k.shape[0]
    kk = k.transpose(0, 2, 1, 3)
    vv = v.reshape(b, nkt, ATT_TILE, NSA_G, NSA_DH).transpose(0, 3, 1, 4, 2)
    return kk, vv


def _nsa_mixer(x, tq, nq, q_pos0, kv_past, win_past, wts, tab, decode):
    (w_in, w_gate, gate_b, pe, w1, w2, w_out) = wts
    n = x.shape[0]
    q, kv_new, win_new = _proj(x, w_in, (D_MODEL, 4 * NSA_G * NSA_DH, 2 * NSA_G * NSA_DH))
    t_new = nq * tq if not decode else 4
    if decode:
        nseq = n // 4
        to_seq = lambda a: a.reshape(4, nseq, -1).transpose(1, 0, 2)
        q, kv_new, win_new = to_seq(q), to_seq(kv_new), to_seq(win_new)
        q = jnp.pad(q, ((0, 0), (0, tq - 4), (0, 0)))
    else:
        nseq = n // t_new
        q, kv_new, win_new = (a.reshape(nseq, t_new, -1) for a in (q, kv_new, win_new))
    kv_new = kv_new.reshape(nseq, t_new, 4, NSA_G, NSA_DH)
    win_new = win_new.reshape(nseq, t_new, 2, NSA_G, NSA_DH)
    kv = kv_new if kv_past is None else jnp.concatenate([kv_past, kv_new], axis=1)
    win = win_new if win_past is None else jnp.concatenate([win_past, win_new], axis=1)
    tk_all = kv.shape[1]
    n_cmp = tk_all // NSA_L_CMP
    n_slc = -(-tk_all // NSA_L_SLC)
    n_slc_p = -(-n_slc // 8) * 8
    lanes = NSA_HG * tq
    q4 = _to_q4(q, nq, tq)

    rows = kv[:, :n_cmp * NSA_L_CMP, 0:2].reshape(nseq, n_cmp, NSA_L_CMP, 2, NSA_G, NSA_DH)
    rows = rows.transpose(3, 0, 1, 4, 2, 5).reshape(2, nseq * n_cmp * NSA_G, NSA_L_CMP * NSA_DH)
    cmp = _compress(rows, pe.reshape(2, 1, NSA_L_CMP * NSA_DH), w1, w2).reshape(2, nseq, n_cmp, NSA_G, NSA_DH)
    kc = cmp[0].transpose(0, 2, 1, 3)
    vct = cmp[1].transpose(0, 2, 3, 1)

    qpos = q_pos0 + jnp.arange(nq * tq, dtype=jnp.int32).reshape(nq, 1, tq)
    blk_end = jnp.arange(n_cmp, dtype=jnp.int32) * NSA_L_CMP + NSA_L_CMP - 1
    dist = qpos - blk_end[None, :, None]
    bias_c = _bias_table(tab, dist, dist >= 0, tq)
    pair = (jnp.arange(n_cmp)[None, :] // (NSA_L_SLC // NSA_L_CMP) == jnp.arange(n_slc_p)[:, None]).astype(F32)
    qpos4 = jnp.tile(qpos, (1, 1, NSA_HG))
    oc4, sel = _cmp_attn(q4, kc, vct, bias_c, pair, qpos4, tq, n_slc)

    tk = ATT_TILE
    nkt = -(-tk_all // tk)
    padk = lambda a: jnp.pad(a, ((0, 0), (0, nkt * tk - a.shape[1]), (0, 0), (0, 0)))
    ks, vst = _kv_layout(padk(kv[:, :, 2]), padk(kv[:, :, 3]), nkt)
    et = (jnp.arange(nkt * tk)[:, None] // NSA_L_SLC == jnp.arange(n_slc_p)[None, :]).astype(BF16)
    et = et.reshape(nkt, tk, n_slc_p)
    rq = jnp.arange(tq, dtype=jnp.int32)[None, None, :]
    rk = jnp.arange(tk, dtype=jnp.int32)[None, :, None]
    if decode:
        kpos = (jnp.arange(nkt, dtype=jnp.int32) * tk)[:, None, None] + rk
        d_s = q_pos0 + rq - kpos
        bsel = _bias_table(tab, d_s, (d_s >= 0) & (kpos < tk_all), tq)
        w_len = win.shape[1]
        nwt = -(-w_len // tk)
        kposw = q_pos0 + 4 - w_len + (jnp.arange(nwt, dtype=jnp.int32) * tk)[:, None, None] + rk
        d_w = q_pos0 + rq - kposw
        bwin = _bias_table(tab, d_w, (d_w >= 0) & (d_w < NSA_WINDOW) & (kposw >= 0) & (kposw < q_pos0 + 4), tq)
        padw = lambda a: jnp.pad(a, ((0, 0), (0, nwt * tk - w_len), (0, 0), (0, 0)))
        kw, vwt = _kv_layout(padw(win[:, :, 0]), padw(win[:, :, 1]), nwt)
    else:
        off_s = jnp.arange(3, dtype=jnp.int32)[:, None, None] * tk
        d_s = off_s + rq - rk
        bsel = _bias_table(tab, d_s, d_s >= 0, tq)
        off_w = jnp.arange(NSA_WINDOW // tk + 1, dtype=jnp.int32)[:, None, None] * tk
        d_w = off_w + rq - rk
        bwin = _bias_table(tab, d_w, (d_w >= 0) & (d_w < NSA_WINDOW), tq)
        kw, vwt = _kv_layout(win[:, :, 0], win[:, :, 1], nkt)
    os4, ow4 = _sw_attn(q4, ks, vst, sel, et, bsel, kw, vwt, bwin, decode)

    if decode:
        un4 = lambda o4: _from_q4(o4, tq).reshape(D_MODEL, nseq, tq)[:, :, :4].transpose(0, 2, 1).reshape(D_MODEL, n)
    else:
        un4 = lambda o4: _from_q4(o4, tq)
    y = _nsa_out(x, un4(oc4), un4(os4), un4(ow4), w_gate, gate_b, w_out)
    return y, kv_new, win_new


def kernel(x_prompt, x_sample, state_gdn_s, state_gdn_conv, cache_nsa_kv, cache_nsa_win, page_table, p_prompt,
           p_sample, ln_g, ln_b, ffn_w_up, ffn_w_down, ple_w_gate, ple_w_proj, a_w_in, a_ln_g, a_ln_b, a_w_s, a_b_s,
           a_w_out, gdn_w_in, gdn_conv_w, gdn_a_log, gdn_dt_bias, gdn_norm_g, gdn_w_out, nsa_w_in, nsa_gate_b,
           nsa_cmp_pe, nsa_cmp_w1, nsa_cmp_w2, nsa_w_out, t5_bias_table):
    bp, tp, _ = x_prompt.shape
    bs, ts, _ = x_sample.shape
    bf = lambda a: a.astype(BF16)
    row = lambda a: a.reshape(1, -1)
    xp = x_prompt.reshape(bp * tp, D_MODEL)
    xs = x_sample.transpose(1, 0, 2).reshape(ts * bs, D_MODEL)
    pp = p_prompt.reshape(DEPTH, bp * tp, -1)
    ps = p_sample.transpose(0, 2, 1, 3).reshape(DEPTH, ts * bs, -1)
    outs = {k: [] for k in ("a_v", "s_p", "c_p", "s_s", "c_s", "kv_p", "win_p", "kv_s", "win_s")}
    ia = ib = ic = 0
    for i in range(DEPTH):
        wu, wd = bf(ffn_w_up[i, 0]), bf(ffn_w_down[i, 0])
        xp = _ffn(xp, wu, wd, row(ln_g[i, 0]), row(ln_b[i, 0]))
        xs = _ffn(xs, wu, wd, row(ln_g[i, 0]), row(ln_b[i, 0]))
        kind = i % 3
        if kind == 0:
            causal = jnp.tril(jnp.ones((A_CHUNK, A_CHUNK), bool))
            ws = bf(jnp.where(causal, a_w_s[ia], 0.0))
            bexp = jnp.repeat(a_b_s[ia].T, 128, axis=1)
            win, wout = bf(a_w_in[ia]), bf(a_w_out[ia])
            lg, lb = row(a_ln_g[ia]), row(a_ln_b[ia])
            (yp,) = _mix_a(xp, win, lg, lb, ws, bexp, wout, decode=False)
            ws_dec = jnp.repeat(a_w_s[ia][:, :ts, :ts].transpose(1, 2, 0).reshape(ts * ts, A_GROUPS), 128, axis=1)
            ys, v_rows = _mix_a(xs, win, lg, lb, ws_dec, bexp[:8], wout, decode=True)
            outs["a_v"].append(v_rows.reshape(ts, bs, A_INNER).transpose(1, 0, 2))
            ia += 1
        elif kind == 1:
            w = gdn_w_in[ib]
            wqkv, wz = bf(w[:, :GDN_QKV]), bf(w[:, GDN_QKV:GDN_QKV + D_MODEL])
            wba = bf(jnp.pad(w[:, GDN_QKV + D_MODEL:], ((0, 0), (0, 128 - 2 * GDN_H))))
            arow = jnp.pad(-jnp.exp(gdn_a_log[ib]), (GDN_H, 128 - 2 * GDN_H)).reshape(1, 128)
            dtrow = jnp.pad(gdn_dt_bias[ib], (GDN_H, 128 - 2 * GDN_H)).reshape(1, 128)
            ng, wout = row(jnp.tile(gdn_norm_g[ib], GDN_H)), bf(gdn_w_out[ib])
            q, k, v, z, ba, tail = _gdn_pre(xp, wqkv, wz, wba, gdn_conv_w[ib], arow, dtrow, seq_len=tp)
            o, s_p = _gdn_scan(q, k, v, ba, jnp.zeros((bp, GDN_H, GDN_D, GDN_D), F32), GDN_CHUNK)
            yp = _gdn_out(o, z, ng, wout)
            outs["s_p"].append(s_p)
            outs["c_p"].append(tail[:, 8 - (GDN_CONV - 1):])
            buf_t = state_gdn_conv[ib].transpose(1, 0, 2).reshape((GDN_CONV - 1) * bs, GDN_QKV)
            q, k, v, z, ba, nbuf = _gdn_pre(xs, wqkv, wz, wba, gdn_conv_w[ib], arow, dtrow, conv_buf=buf_t)
            seq8 = lambda a: jnp.pad(a.reshape(ts, bs, -1).transpose(1, 0, 2), ((0, 0), (0, 8 - ts), (0, 0))).reshape(
                bs * 8, -1)
            o, s_s = _gdn_scan(seq8(q), seq8(k), seq8(v), seq8(ba), state_gdn_s[ib], 8)
            o = o.reshape(bs, 8, D_MODEL)[:, :ts].transpose(1, 0, 2).reshape(ts * bs, D_MODEL)
            ys = _gdn_out(o, z, ng, wout)
            outs["s_s"].append(s_s)
            outs["c_s"].append(nbuf.reshape(GDN_CONV - 1, bs, GDN_QKV).transpose(1, 0, 2))
            ib += 1
        else:
            w = nsa_w_in[ic]
            n_att = NSA_H * NSA_DH + 6 * NSA_G * NSA_DH
            wts = (bf(w[:, :n_att]), bf(w[:, n_att:].T), nsa_gate_b[ic].reshape(-1, 1), nsa_cmp_pe[ic],
                   bf(nsa_cmp_w1[ic]), bf(nsa_cmp_w2[ic]), bf(nsa_w_out[ic]))
            yp, kvn_p, winn_p = _nsa_mixer(xp, ATT_TILE, tp // ATT_TILE, 0, None, None, wts, t5_bias_table, False)
            n_pages = page_table.shape[1]
            past_len = n_pages * cache_nsa_kv.shape[2]
            kv_past = cache_nsa_kv[ic][page_table].reshape(bs, past_len, 4, NSA_G, NSA_DH)
            ys, kvn_s, winn_s = _nsa_mixer(xs, 32, 1, past_len, kv_past, cache_nsa_win[ic], wts, t5_bias_table, True)
            outs["kv_p"].append(kvn_p)
            outs["win_p"].append(winn_p[:, tp - min(NSA_WINDOW, tp):])
            outs["kv_s"].append(kvn_s)
            outs["win_s"].append(winn_s)
            ic += 1
        wu, wd = bf(ffn_w_up[i, 1]), bf(ffn_w_down[i, 1])
        wg, wp = bf(ple_w_gate[i]), bf(ple_w_proj[i])
        pre = lambda y: (y, row(ln_g[i, 1]), row(ln_b[i, 1]))
        xp = _ffn(xp, wu, wd, row(ln_g[i, 2]), row(ln_b[i, 2]), pre=pre(yp), ple=(pp[i], wg, wp))
        xs = _ffn(xs, wu, wd, row(ln_g[i, 2]), row(ln_b[i, 2]), pre=pre(ys), ple=(ps[i], wg, wp))
    st = lambda k: jnp.stack(outs[k])
    return (xp.reshape(bp, tp, D_MODEL), xs.reshape(ts, bs, D_MODEL).transpose(1, 0, 2), st("a_v"), st("s_p"),
            st("c_p"), st("s_s"), st("c_s"), st("kv_p"), st("win_p"), st("kv_s"), st("win_s"))
```

```python
import functools
import math

import jax
import jax.numpy as jnp
from jax import lax
from jax.experimental import pallas as pl
from jax.experimental.pallas import tpu as pltpu

F32 = jnp.float32
BF16 = jnp.bfloat16
HIGHEST = lax.Precision.HIGHEST

D_MODEL = 1024
DEPTH = 4
ALPHA = (2 * DEPTH) ** 0.25
LN_EPS = 1e-5
NORM_EPS = 1e-6
D_FF = 2816
A_CHUNK = 128
A_INNER = 2 * D_MODEL
A_GROUPS = 16
GDN_H = 8
GDN_D = 128
GDN_QKV = 3 * D_MODEL
GDN_CONV = 4
GDN_CHUNK = 64
NSA_H = 16
NSA_DH = 64
NSA_G = 4
NSA_HG = 4
NSA_L_CMP = 32
NSA_L_SLC = 64
NSA_N_SEL = 16
NSA_WINDOW = 512
NSA_FORCE = 100.0
NEG_INF = -1e30
NUM_BUCKETS = 32
MAX_DISTANCE = 128

ROW_TILE = 256
ATT_TILE = 128
VMEM_LIMIT = 56 << 20


def _cparams(*sem):
    return pltpu.CompilerParams(dimension_semantics=sem, vmem_limit_bytes=VMEM_LIMIT)


def _resident(shape):
    nd = len(shape)
    return pl.BlockSpec(shape, lambda *_: (0,) * nd, pipeline_mode=pl.Buffered(1))


def _rows(tm, width):
    return pl.BlockSpec((tm, width), lambda i: (i, 0))


def _bdot(a, b):
    return jnp.dot(a.astype(BF16), b.astype(BF16), preferred_element_type=F32)


def _bdot_nt(a, b):
    return lax.dot_general(a.astype(BF16), b.astype(BF16), (((1,), (1,)), ((), ())),
                           preferred_element_type=F32)


def _bdot_tn(a, b):
    return lax.dot_general(a.astype(BF16), b.astype(BF16), (((0,), (0,)), ((), ())),
                           preferred_element_type=F32)


def _hdot(a, b):
    return jnp.dot(a, b, precision=HIGHEST, preferred_element_type=F32)


def _sigmoid(x):
    return 1.0 / (1.0 + jnp.exp(-x))


def _silu(x):
    return x * _sigmoid(x)


def _gelu(x):
    return 0.5 * x * (1.0 + jnp.tanh(math.sqrt(2.0 / math.pi) * (x + 0.044715 * (x * x * x))))


def _softplus(x):
    return jnp.maximum(x, 0.0) + jnp.log(1.0 + jnp.exp(-jnp.abs(x)))


def _ln(y, g, b):
    mu = jnp.mean(y, axis=-1, keepdims=True)
    d = y - mu
    var = jnp.mean(d * d, axis=-1, keepdims=True)
    return d * lax.rsqrt(var + LN_EPS) * g + b


def _ffn_body(*refs, pre_norm, ple):
    it = iter(refs)
    x_ref = next(it)
    if pre_norm:
        y_ref, g1, b1 = next(it), next(it), next(it)
    wu, wd, g2, b2 = next(it), next(it), next(it), next(it)
    if ple:
        p_ref, wg, wp = next(it), next(it), next(it)
    o_ref = next(it)
    x = x_ref[...]
    if pre_norm:
        x = _ln(ALPHA * x + y_ref[...], g1[...], b1[...])
    h = _bdot(x, wu[...])
    act = _silu(h[:, :D_FF]) * h[:, D_FF:]
    x = _ln(ALPHA * x + 0.5 * _bdot(act, wd[...]), g2[...], b2[...])
    if ple:
        x = x + _sigmoid(_bdot(x, wg[...])) * _bdot(p_ref[...], wp[...])
    o_ref[...] = x


def _ffn(x, wu, wd, g2, b2, pre=None, ple=None):
    n = x.shape[0]
    tm = ROW_TILE
    args, specs = [x], [_rows(tm, D_MODEL)]
    if pre is not None:
        y, g1, b1 = pre
        args += [y, g1, b1]
        specs += [_rows(tm, D_MODEL), _resident((1, D_MODEL)), _resident((1, D_MODEL))]
    args += [wu, wd, g2, b2]
    specs += [_resident(wu.shape), _resident(wd.shape), _resident((1, D_MODEL)), _resident((1, D_MODEL))]
    if ple is not None:
        p, wg, wp = ple
        args += [p, wg, wp]
        specs += [_rows(tm, p.shape[1]), _resident(wg.shape), _resident(wp.shape)]
    return pl.pallas_call(
        functools.partial(_ffn_body, pre_norm=pre is not None, ple=ple is not None),
        grid=(n // tm,), in_specs=specs, out_specs=_rows(tm, D_MODEL),
        out_shape=jax.ShapeDtypeStruct((n, D_MODEL), F32),
        compiler_params=_cparams("parallel"), name="ffn")(*args)


def _mixa_body(x_ref, win, lg, lb, ws, bs, wout, y_ref, *rest, decode):
    if decode:
        v_ref, s_ref = rest
    else:
        (s_ref,) = rest
    tm = x_ref.shape[0]
    h = _gelu(_bdot(x_ref[...], win[...]))
    u = h[:, :A_INNER]
    v = _ln(h[:, A_INNER:], lg[...], lb[...])
    if decode:
        v_ref[...] = v
        nb = tm // 4
        for t in range(4):
            acc = bs[t:t + 1, :]
            for s in range(t + 1):
                acc = acc + ws[4 * t + s:4 * t + s + 1, :] * v[s * nb:(s + 1) * nb, :]
            s_ref[t * nb:(t + 1) * nb, :] = acc
    else:
        for c in range(tm // A_CHUNK):
            r = slice(c * A_CHUNK, (c + 1) * A_CHUNK)
            for g in range(A_GROUPS):
                l = slice(g * 128, (g + 1) * 128)
                s_ref[r, l] = jnp.dot(ws[g], v[r, l].astype(BF16), preferred_element_type=F32) + bs[:, l]
    y_ref[...] = _bdot(u * s_ref[...], wout[...])


def _mix_a(x, win, lg, lb, ws, bs, wout, decode):
    n = x.shape[0]
    tm = n if decode else ROW_TILE
    out_shape = [jax.ShapeDtypeStruct((n, D_MODEL), F32)]
    out_specs = [_rows(tm, D_MODEL)]
    if decode:
        out_shape.append(jax.ShapeDtypeStruct((n, A_INNER), F32))
        out_specs.append(_rows(tm, A_INNER))
    res = pl.pallas_call(
        functools.partial(_mixa_body, decode=decode),
        grid=(n // tm,),
        in_specs=[_rows(tm, D_MODEL), _resident(win.shape), _resident((1, A_INNER)), _resident((1, A_INNER)),
                  _resident(ws.shape), _resident(bs.shape), _resident(wout.shape)],
        out_specs=out_specs, out_shape=out_shape,
        scratch_shapes=[pltpu.VMEM((tm, A_INNER), F32)],
        compiler_params=_cparams("parallel"), name="mix_a_dec" if decode else "mix_a")(x, win, lg, lb, ws, bs, wout)
    return res


def _gdn_post(act, x, wz, wba, arow, dtrow, q_o, k_o, v_o, z_o, ba_o):
    for h in range(GDN_H):
        l = slice(h * GDN_D, (h + 1) * GDN_D)
        qh = act[:, l]
        q_o[:, l] = qh * lax.rsqrt(jnp.sum(qh * qh, axis=-1, keepdims=True) + NORM_EPS) * (GDN_D ** -0.5)
        kh = act[:, D_MODEL + h * GDN_D:D_MODEL + (h + 1) * GDN_D]
        k_o[:, l] = kh * lax.rsqrt(jnp.sum(kh * kh, axis=-1, keepdims=True) + NORM_EPS)
    v_o[...] = act[:, 2 * D_MODEL:]
    z_o[...] = _bdot(x, wz[...])
    lg = _bdot(x, wba[...])
    lane = lax.broadcasted_iota(jnp.int32, lg.shape, 1)
    ba_o[...] = jnp.where(lane < GDN_H, _sigmoid(lg), arow[...] * _softplus(lg + dtrow[...]))


def _gdn_pre_body(x_ref, xh_ref, wqkv, wz, wba, cw, arow, dtrow, q_o, k_o, v_o, z_o, ba_o, tail_o, ext,
                  *, tiles_per_seq):
    tm = x_ref.shape[0]
    x = x_ref[...]
    pre = _bdot(x, wqkv[...])
    halo = _bdot(xh_ref[...], wqkv[...])
    first = pl.program_id(0) % tiles_per_seq == 0
    ext[0:8, :] = jnp.where(first, 0.0, halo)
    ext[8:8 + tm, :] = pre
    tail_o[0] = pre[tm - 8:, :]
    conv = ext[pl.ds(5, tm), :] * cw[0:1, :]
    for j in range(1, GDN_CONV):
        conv = conv + ext[pl.ds(5 + j, tm), :] * cw[j:j + 1, :]
    _gdn_post(_silu(conv), x, wz, wba, arow, dtrow, q_o, k_o, v_o, z_o, ba_o)


def _gdn_pre_dec_body(x_ref, buf_ref, wqkv, wz, wba, cw, arow, dtrow, q_o, k_o, v_o, z_o, ba_o, nbuf_o, ext):
    n = x_ref.shape[0]
    nb = n // 4
    x = x_ref[...]
    ext[0:3 * nb, :] = buf_ref[...]
    ext[3 * nb:3 * nb + n, :] = _bdot(x, wqkv[...])
    nbuf_o[...] = ext[4 * nb:7 * nb, :]
    conv = ext[0:n, :] * cw[0:1, :]
    for j in range(1, GDN_CONV):
        conv = conv + ext[j * nb:j * nb + n, :] * cw[j:j + 1, :]
    _gdn_post(_silu(conv), x, wz, wba, arow, dtrow, q_o, k_o, v_o, z_o, ba_o)


def _gdn_pre(x, wqkv, wz, wba, cw, arow, dtrow, seq_len=None, conv_buf=None):
    n = x.shape[0]
    w_specs = [_resident(wqkv.shape), _resident(wz.shape), _resident(wba.shape), _resident(cw.shape),
               _resident((1, 128)), _resident((1, 128))]
    outs = [jax.ShapeDtypeStruct((n, D_MODEL), F32)] * 4 + [jax.ShapeDtypeStruct((n, 128), F32)]
    if conv_buf is None:
        tm = ROW_TILE
        tps = seq_len // tm
        nseq = n // seq_len
        return pl.pallas_call(
            functools.partial(_gdn_pre_body, tiles_per_seq=tps), grid=(n // tm,),
            in_specs=[_rows(tm, D_MODEL),
                      pl.BlockSpec((8, D_MODEL), lambda i: (jnp.maximum(i * (tm // 8) - 1, 0), 0))] + w_specs,
            out_specs=[_rows(tm, D_MODEL)] * 4 + [_rows(tm, 128),
                                                   pl.BlockSpec((1, 8, GDN_QKV), lambda i: (i // tps, 0, 0))],
            out_shape=outs + [jax.ShapeDtypeStruct((nseq, 8, GDN_QKV), F32)],
            scratch_shapes=[pltpu.VMEM((tm + 8, GDN_QKV), F32)],
            compiler_params=_cparams("arbitrary"), name="gdn_pre")(x, x, wqkv, wz, wba, cw, arow, dtrow)
    nb = n // 4
    return pl.pallas_call(
        _gdn_pre_dec_body, grid=(1,),
        in_specs=[_rows(n, D_MODEL), _rows(3 * nb, GDN_QKV)] + w_specs,
        out_specs=[_rows(n, D_MODEL)] * 4 + [_rows(n, 128), _rows(3 * nb, GDN_QKV)],
        out_shape=outs + [jax.ShapeDtypeStruct((3 * nb, GDN_QKV), F32)],
        scratch_shapes=[pltpu.VMEM((7 * nb, GDN_QKV), F32)],
        compiler_params=_cparams("arbitrary"), name="gdn_pre_dec")(x, conv_buf, wqkv, wz, wba, cw, arow, dtrow)


def _gdn_scan_body(q_ref, k_ref, v_ref, ba_ref, s0_ref, o_ref, s_ref):
    c = q_ref.shape[0]
    heads = range(GDN_H)

    @pl.when(pl.program_id(1) == 0)
    def _():
        s_ref[...] = s0_ref[...]

    ba = ba_ref[...]
    ri = lax.broadcasted_iota(jnp.int32, (c, c), 0)
    ci = lax.broadcasted_iota(jnp.int32, (c, c), 1)
    incl = ci <= ri
    strict = ci < ri
    eye = (ri == ci).astype(F32)
    gcum_all = _hdot(incl.astype(F32), ba)
    lanes = [slice(h * GDN_D, (h + 1) * GDN_D) for h in heads]
    gc, decay, kbeta, inv, pw = [], [], [], [], []
    for h in heads:
        kh = k_ref[:, lanes[h]]
        g = gcum_all[:, GDN_H + h:GDN_H + h + 1]
        g_row = jnp.sum(g * eye, axis=0, keepdims=True)
        d = jnp.where(incl, jnp.exp(jnp.where(incl, g - g_row, 0.0)), 0.0)
        kb = kh * ba[:, h:h + 1]
        a = jnp.where(strict, _bdot_nt(kb, kh) * d, 0.0)
        gc.append(g), decay.append(d), kbeta.append(kb), inv.append(eye - a), pw.append(a)
    for _ in range(int(math.log2(c)) - 1):
        pw = [_hdot(p, p) for p in pw]
        inv = [t + _hdot(t, p) for t, p in zip(inv, pw)]
    u = [_bdot(inv[h], v_ref[:, lanes[h]] * ba[:, h:h + 1]) for h in heads]
    w = [_bdot(inv[h], kbeta[h] * jnp.exp(gc[h])) for h in heads]
    qk = [jnp.where(incl, _bdot_nt(q_ref[:, lanes[h]], k_ref[:, lanes[h]]) * decay[h], 0.0) for h in heads]
    s = [s_ref[0, h] for h in heads]
    v_new = [u[h] - _bdot(w[h], s[h]) for h in heads]
    for h in heads:
        o_ref[:, lanes[h]] = _bdot(q_ref[:, lanes[h]] * jnp.exp(gc[h]), s[h]) + _bdot(qk[h], v_new[h])
    for h in heads:
        g_last = gc[h][c - 1:c, :]
        kd = k_ref[:, lanes[h]] * jnp.exp(g_last - gc[h])
        s_ref[0, h] = s[h] * jnp.exp(g_last) + _bdot_tn(kd, v_new[h])


def _gdn_scan(q, k, v, ba, s0, chunk):
    n = q.shape[0]
    nseq = s0.shape[0]
    nch = n // nseq // chunk
    row = lambda w: pl.BlockSpec((chunk, w), lambda b, c: (b * nch + c, 0))
    st = pl.BlockSpec((1, GDN_H, GDN_D, GDN_D), lambda b, c: (b, 0, 0, 0))
    return pl.pallas_call(
        _gdn_scan_body, grid=(nseq, nch),
        in_specs=[row(D_MODEL), row(D_MODEL), row(D_MODEL), row(128), st],
        out_specs=[row(D_MODEL), st],
        out_shape=[jax.ShapeDtypeStruct((n, D_MODEL), F32), jax.ShapeDtypeStruct(s0.shape, F32)],
        compiler_params=_cparams("parallel", "arbitrary"), name="gdn_scan")(q, k, v, ba, s0)


def _gdn_out_body(o_ref, z_ref, ng, wout, y_ref, t_ref):
    for h in range(GDN_H):
        l = slice(h * GDN_D, (h + 1) * GDN_D)
        oh = o_ref[:, l]
        t_ref[:, l] = oh * lax.rsqrt(jnp.mean(oh * oh, axis=-1, keepdims=True) + NORM_EPS)
    y_ref[...] = _bdot(t_ref[...] * ng[...] * _silu(z_ref[...]), wout[...])


def _gdn_out(o, z, ng, wout):
    n = o.shape[0]
    tm = ROW_TILE
    return pl.pallas_call(
        _gdn_out_body, grid=(n // tm,),
        in_specs=[_rows(tm, D_MODEL), _rows(tm, D_MODEL), _resident((1, D_MODEL)), _resident(wout.shape)],
        out_specs=_rows(tm, D_MODEL), out_shape=jax.ShapeDtypeStruct((n, D_MODEL), F32),
        scratch_shapes=[pltpu.VMEM((tm, D_MODEL), F32)],
        compiler_params=_cparams("parallel"), name="gdn_out")(o, z, ng, wout)


NSA_SCALE = NSA_DH ** -0.5
PAGE = 128


def _projt_body(x_ref, wt_ref, *o_refs):
    yt = _bdot_nt(wt_ref[...], x_ref[...])
    off = 0
    for o in o_refs:
        wdt = o.shape[1]
        o[0] = yt[off:off + wdt, :]
        off += wdt


def _projt(x, wt, widths, seq_len):
    n = x.shape[0]
    tm = ROW_TILE
    tps = seq_len // tm
    return pl.pallas_call(
        _projt_body, grid=(n // tm,),
        in_specs=[_rows(tm, D_MODEL), _resident(wt.shape)],
        out_specs=[pl.BlockSpec((1, wd, tm), lambda i: (i // tps, 0, i % tps)) for wd in widths],
        out_shape=[jax.ShapeDtypeStruct((n // seq_len, wd, seq_len), F32) for wd in widths],
        compiler_params=_cparams("parallel"), name="nsa_proj")(x, wt)


def _cmp_body(*refs, n_pages, group, paged):
    if paged:
        refs = refs[1:]
    srcs, (pet, w1, w2, o_ref, tall, flat) = refs[:-6], refs[-6:]
    n_blk = n_pages * (PAGE // NSA_L_CMP)
    slot = pl.program_id(0) % group
    row0 = pl.multiple_of(slot * n_blk, 8)
    for c in range(2):
        for gp in range(2):
            for p in range(n_pages):
                if paged:
                    slab = srcs[p][0, 0, c, 2 * gp:2 * gp + 2].reshape(2 * NSA_DH, PAGE)
                else:
                    slab = srcs[0][0, (2 * c + gp) * 128:(2 * c + gp + 1) * 128, p * PAGE:(p + 1) * PAGE]
                tall[p * PAGE:(p + 1) * PAGE, :] = (slab + pet[c]).T
            for l in range(NSA_L_CMP):
                flat[2 * c + gp, pl.ds(row0, n_blk), l * 128:(l + 1) * 128] = tall[pl.ds(l, n_blk, stride=NSA_L_CMP), :]

    @pl.when(slot == group - 1)
    def _():
        for c in range(2):
            for gp in range(2):
                hid = _gelu(_bdot(flat[2 * c + gp], w1[c]))
                o_ref[c, :, gp * 128:(gp + 1) * 128] = _bdot(hid, w2[c])


def _compress(src, pet, w1bd, w2bd, page_table=None):
    paged = page_table is not None
    if paged:
        nseq, n_pages = page_table.shape
        group = 4
        src_specs = [pl.BlockSpec((1, 1, 2, NSA_G, NSA_DH, PAGE),
                                  functools.partial(lambda b, pt, p: (0, pt[b, p], 0, 0, 0, 0), p=p))
                     for p in range(n_pages)]
        srcs = [src] * n_pages
        wmap = lambda nd: (lambda b, pt: (0,) * nd)
        omap = lambda b, pt: (0, b // group, 0)
    else:
        nseq, _, t = src.shape
        n_pages = t // PAGE
        group = 1
        src_specs = [pl.BlockSpec((1, 2 * NSA_G * NSA_DH, t), lambda b: (b, 0, 0))]
        srcs = [src]
        wmap = lambda nd: (lambda b: (0,) * nd)
        omap = lambda b: (0, b, 0)
    n_blk = n_pages * (PAGE // NSA_L_CMP)
    in_specs = src_specs + [pl.BlockSpec(a.shape, wmap(a.ndim), pipeline_mode=pl.Buffered(1)) for a in (pet, w1bd, w2bd)]
    out_spec = pl.BlockSpec((2, group * n_blk, NSA_G * NSA_DH), omap)
    scratch = [pltpu.VMEM((n_pages * PAGE, 128), F32), pltpu.VMEM((4, group * n_blk, NSA_L_CMP * 128), F32)]
    out_shape = jax.ShapeDtypeStruct((2, nseq * n_blk, NSA_G * NSA_DH), F32)
    body = functools.partial(_cmp_body, n_pages=n_pages, group=group, paged=paged)
    if paged:
        gs = pltpu.PrefetchScalarGridSpec(num_scalar_prefetch=1, grid=(nseq,), in_specs=in_specs, out_specs=out_spec,
                                          scratch_shapes=scratch)
        return pl.pallas_call(body, grid_spec=gs, out_shape=out_shape, compiler_params=_cparams("arbitrary"),
                              name="nsa_compress_paged")(page_table, *srcs, pet, w1bd, w2bd)
    return pl.pallas_call(body, grid=(nseq,), in_specs=in_specs, out_specs=out_spec, out_shape=out_shape,
                          scratch_shapes=scratch, compiler_params=_cparams("arbitrary"),
                          name="nsa_compress")(*srcs, pet, w1bd, w2bd)


def _select(ps, jq, n_slc):
    n_p, w = ps.shape
    j = lax.broadcasted_iota(jnp.int32, (n_p, w), 0)
    forced = (j == 0) | (j == jq) | (j == jq - 1)
    sc = jnp.where(forced, NSA_FORCE, jnp.where(j > jq, -1.0, ps))
    sc = jnp.where(j >= n_slc, -2.0, sc)
    rank = jnp.zeros((n_p, w), F32)
    for i in range(n_slc):
        r = sc[i:i + 1, :]
        rank = rank + jnp.where((r > sc) | ((r == sc) & (j > i)), 1.0, 0.0)
    return jnp.where(rank < NSA_N_SEL, 1.0, 0.0)


def _softmax_keys(s, mask):
    m = jnp.max(s, axis=0, keepdims=True)
    e = jnp.exp(s - m)
    return jnp.where(mask, e / jnp.sum(e, axis=0, keepdims=True), 0.0)


def _q4(q_ref, g0, tq):
    return jnp.concatenate([q_ref[0, (g0 + hh) * NSA_DH:(g0 + hh + 1) * NSA_DH, :] for hh in range(NSA_HG)], axis=1)


def _cmp_attn_body(q_ref, kc_ref, vc_ref, b_ref, pair_ref, o_ref, sel_ref, *, n_slc):
    tq = q_ref.shape[-1]
    lanes = NSA_HG * tq
    kc, vc = kc_ref[0], vc_ref[0]
    qpos = pl.program_id(1) * tq + lax.broadcasted_iota(jnp.int32, (1, tq), 1)
    for g in range(NSA_G):
        f = slice(g * NSA_DH, (g + 1) * NSA_DH)
        bias = b_ref[0, :, g * lanes:(g + 1) * lanes]
        mask = bias > 0.5 * NEG_INF
        s = jnp.where(mask, _bdot(kc[:, f], _q4(q_ref, g * NSA_HG, tq)) * NSA_SCALE + bias, NEG_INF)
        p = _softmax_keys(s, mask)
        o = _bdot_tn(vc[:, f], p)
        ps = p[:, :tq]
        for hh in range(NSA_HG):
            o_ref[0, (g * NSA_HG + hh) * NSA_DH:(g * NSA_HG + hh + 1) * NSA_DH, :] = o[:, hh * tq:(hh + 1) * tq]
            if hh:
                ps = ps + p[:, hh * tq:(hh + 1) * tq]
        sel_ref[0, g] = _select(_hdot(pair_ref[...], ps), qpos // NSA_L_SLC, n_slc)


def _cmp_attn(qt, kc, vc, bias, pair, n_slc):
    b, _, t = qt.shape
    tq = ATT_TILE
    n_cmp = kc.shape[1]
    n_p = pair.shape[0]
    qs = pl.BlockSpec((1, D_MODEL, tq), lambda b_, i: (b_, 0, i))
    ks = pl.BlockSpec((1, n_cmp, NSA_G * NSA_DH), lambda b_, i: (b_, 0, 0))
    return pl.pallas_call(
        functools.partial(_cmp_attn_body, n_slc=n_slc), grid=(b, t // tq),
        in_specs=[qs, ks, ks, pl.BlockSpec((1, n_cmp, NSA_H * tq), lambda b_, i: (i, 0, 0)),
                  pl.BlockSpec(pair.shape, lambda b_, i: (0, 0))],
        out_specs=[qs, pl.BlockSpec((1, NSA_G, n_p, tq), lambda b_, i: (b_, 0, 0, i))],
        out_shape=[jax.ShapeDtypeStruct(qt.shape, F32), jax.ShapeDtypeStruct((b, NSA_G, n_p, t), F32)],
        compiler_params=_cparams("parallel", "parallel"), name="nsa_cmp_attn")(qt, kc, vc, bias, pair)


def _flash(q, kt, vt, bias, sel_mask, m, l, acc):
    mask = bias > 0.5 * NEG_INF
    if sel_mask is not None:
        mask = mask & sel_mask
    s = jnp.where(mask, _bdot_tn(kt, q) * NSA_SCALE + bias, NEG_INF)
    m_new = jnp.maximum(m, jnp.max(s, axis=0, keepdims=True))
    alpha = jnp.exp(m - m_new)
    p = jnp.where(mask, jnp.exp(s - m_new), 0.0)
    return m_new, alpha * l + jnp.sum(p, axis=0, keepdims=True), alpha * acc + _bdot(vt, p)


def _flash_init(rows, lanes):
    return jnp.full((1, lanes), NEG_INF, F32), jnp.zeros((1, lanes), F32), jnp.zeros((rows, lanes), F32)


def _flash_out(l, acc):
    return acc / jnp.where(l > 0.0, l, 1.0)


def _sw_attn_body(q_ref, ks_ref, vs_ref, sel_ref, et_ref, bs_ref, kw_ref, vw_ref, bw_ref, os_ref, ow_ref,
                  m_s, l_s, acc_s):
    tk = ATT_TILE
    tq = q_ref.shape[-1]
    qt = pl.program_id(2)
    q = _q4(q_ref, 0, tq).astype(BF16)
    sel = jnp.concatenate([sel_ref[0, 0]] * NSA_HG, axis=1).astype(BF16)
    n_sel_bias = bs_ref.shape[0]

    def put(state):
        m_s[...], l_s[...], acc_s[...] = state

    def write(o_ref):
        o = _flash_out(l_s[...], acc_s[...])
        for hh in range(NSA_HG):
            o_ref[0, hh * NSA_DH:(hh + 1) * NSA_DH, :] = o[:, hh * tq:(hh + 1) * tq]

    put(_flash_init(NSA_DH, NSA_HG * tq))

    def sel_step(kt, carry):
        keys = pl.ds(pl.multiple_of(kt * tk, tk), tk)
        sel_mask = jnp.dot(et_ref[kt], sel, preferred_element_type=F32) > 0.5
        put(_flash(q, ks_ref[0, :, keys], vs_ref[0, :, keys], bs_ref[jnp.minimum(qt - kt, n_sel_bias - 1)], sel_mask,
                   m_s[...], l_s[...], acc_s[...]))
        return carry

    lax.fori_loop(0, qt + 1, sel_step, 0)
    write(os_ref)

    put(_flash_init(NSA_DH, NSA_HG * tq))
    for jw in range(bw_ref.shape[0]):
        def win_step(jw=jw):
            keys = pl.ds(pl.multiple_of((qt - jw) * tk, tk), tk)
            put(_flash(q, kw_ref[0, :, keys], vw_ref[0, :, keys], bw_ref[jw], None, m_s[...], l_s[...], acc_s[...]))

        pl.when(qt >= jw)(win_step)
    write(ow_ref)


def _sw_attn(qt, kvt, wint, sel, et, bsel, bwin):
    b, _, t = qt.shape
    tq = tk = ATT_TILE
    lanes = NSA_HG * tq
    qs = pl.BlockSpec((1, NSA_HG * NSA_DH, tq), lambda b_, g_, i: (b_, g_, i))
    feat = lambda blk0: pl.BlockSpec((1, NSA_DH, t), lambda b_, g_, i: (b_, blk0 + g_, 0))
    bspec = lambda a: pl.BlockSpec((a.shape[0], tk, lanes), lambda b_, g_, i: (0, 0, g_))
    return pl.pallas_call(
        _sw_attn_body, grid=(b, NSA_G, t // tq),
        in_specs=[qs, feat(2 * NSA_G), feat(3 * NSA_G),
                  pl.BlockSpec((1, 1, sel.shape[2], tq), lambda b_, g_, i: (b_, g_, 0, i)),
                  pl.BlockSpec(et.shape, lambda b_, g_, i: (0, 0, 0)), bspec(bsel),
                  feat(0), feat(NSA_G), bspec(bwin)],
        out_specs=[qs, qs],
        out_shape=[jax.ShapeDtypeStruct(qt.shape, F32)] * 2,
        scratch_shapes=[pltpu.VMEM((1, lanes), F32), pltpu.VMEM((1, lanes), F32), pltpu.VMEM((NSA_DH, lanes), F32)],
        compiler_params=_cparams("parallel", "parallel", "arbitrary"),
        name="nsa_sw_attn")(qt, kvt, kvt, sel, et, bsel, wint, wint, bwin)


def _dec_attn_body(pt_ref, *refs, n_pages, n_slc, q_pos0, tq):
    pages = refs[:n_pages]
    (q_ref, kc_ref, vc_ref, bc_ref, pair_ref, hsum_ref, knew_ref, vnew_ref, et_ref, bs_ref, win_ref, kwnew_ref,
     vwnew_ref, bw_ref, oc_ref, os_ref, ow_ref) = refs[n_pages:]
    feats, lanes = q_ref.shape[1:]
    q = q_ref[0].astype(BF16)

    bias = bc_ref[...]
    mask = bias > 0.5 * NEG_INF
    p = _softmax_keys(jnp.where(mask, _bdot(kc_ref[0], q) * NSA_SCALE + bias, NEG_INF), mask)
    oc_ref[0] = _bdot_tn(vc_ref[0], p)
    ps = _hdot(pair_ref[...], _hdot(p, hsum_ref[...]))
    lane = lax.broadcasted_iota(jnp.int32, (1, lanes), 1)
    sel = _select(ps, (q_pos0 + lane % tq) // NSA_L_SLC, n_slc).astype(BF16)

    state = _flash_init(feats, lanes)
    for j in range(n_pages + 1):
        if j < n_pages:
            kt, vt = pages[j][0, 0, 0].reshape(feats, PAGE), pages[j][0, 0, 1].reshape(feats, PAGE)
        else:
            kt, vt = knew_ref[0], vnew_ref[0]
        sel_mask = jnp.dot(et_ref[j], sel, preferred_element_type=F32) > 0.5
        state = _flash(q, kt, vt, bs_ref[j], sel_mask, *state)
    os_ref[0] = _flash_out(*state[1:])

    state = _flash_init(feats, lanes)
    n_wt = win_ref.shape[-1] // PAGE
    for j in range(n_wt + 1):
        if j < n_wt:
            kt, vt = win_ref[0, 0, :, j * PAGE:(j + 1) * PAGE], win_ref[0, 1, :, j * PAGE:(j + 1) * PAGE]
        else:
            kt, vt = kwnew_ref[0], vwnew_ref[0]
        state = _flash(q, kt, vt, bw_ref[j], None, *state)
    ow_ref[0] = _flash_out(*state[1:])


def _dec_attn(page_table, cache_t, qbd, kc, vc, bias_c, pair, hsum, knew, vnew, et, bsel, win_t, kwnew, vwnew, bwin,
              n_slc, q_pos0, tq):
    nseq, n_pages = page_table.shape
    per_seq = lambda a: pl.BlockSpec((1,) + a.shape[1:], lambda b, pt: (b,) + (0,) * (a.ndim - 1))
    whole = lambda a: pl.BlockSpec(a.shape, lambda b, pt: (0,) * a.ndim, pipeline_mode=pl.Buffered(1))
    page_specs = [pl.BlockSpec((1, 1, 2, NSA_G, NSA_DH, PAGE),
                               functools.partial(lambda b, pt, p: (0, pt[b, p], 1, 0, 0, 0), p=p))
                  for p in range(n_pages)]
    gs = pltpu.PrefetchScalarGridSpec(
        num_scalar_prefetch=1, grid=(nseq,),
        in_specs=page_specs + [per_seq(qbd), per_seq(kc), per_seq(vc), whole(bias_c), whole(pair), whole(hsum),
                               per_seq(knew), per_seq(vnew), whole(et), whole(bsel), per_seq(win_t), per_seq(kwnew),
                               per_seq(vwnew), whole(bwin)],
        out_specs=[per_seq(qbd)] * 3)
    return pl.pallas_call(
        functools.partial(_dec_attn_body, n_pages=n_pages, n_slc=n_slc, q_pos0=q_pos0, tq=tq), grid_spec=gs,
        out_shape=[jax.ShapeDtypeStruct(qbd.shape, F32)] * 3, compiler_params=_cparams("parallel"),
        name="nsa_dec_attn")(page_table, *([cache_t] * n_pages), qbd, kc, vc, bias_c, pair, hsum, knew, vnew, et, bsel,
                             win_t, kwnew, vwnew, bwin)


def _nsa_out_body(x_ref, oc_ref, os_ref, ow_ref, wg, gb, wout, y_ref):
    gates = _sigmoid(_bdot_nt(wg[...], x_ref[...]) + gb[...])
    parts = []
    for h in range(NSA_H):
        r = slice(h * NSA_DH, (h + 1) * NSA_DH)
        parts.append(gates[h:h + 1, :] * oc_ref[0, r, :] + gates[NSA_H + h:NSA_H + h + 1, :] * os_ref[0, r, :]
                     + gates[2 * NSA_H + h:2 * NSA_H + h + 1, :] * ow_ref[0, r, :])
    y_ref[...] = _bdot_tn(jnp.concatenate(parts, axis=0), wout[...])


def _nsa_out(x, oc, osl, ow, wg, gb, wout):
    n = x.shape[0]
    tm = ROW_TILE
    tps = oc.shape[2] // tm
    col = pl.BlockSpec((1, D_MODEL, tm), lambda i: (i // tps, 0, i % tps))
    return pl.pallas_call(
        _nsa_out_body, grid=(n // tm,),
        in_specs=[_rows(tm, D_MODEL), col, col, col, _resident(wg.shape), _resident(gb.shape), _resident(wout.shape)],
        out_specs=_rows(tm, D_MODEL), out_shape=jax.ShapeDtypeStruct((n, D_MODEL), F32),
        compiler_params=_cparams("parallel"), name="nsa_out")(x, oc, osl, ow, wg, gb, wout)


def _t5_bucket(dist):
    n = jnp.maximum(dist, 0)
    max_exact = NUM_BUCKETS // 2
    nf = jnp.maximum(n, 1).astype(F32)
    large = max_exact + (jnp.log(nf / max_exact) / math.log(MAX_DISTANCE / max_exact)
                         * (NUM_BUCKETS - max_exact)).astype(jnp.int32)
    return jnp.where(n < max_exact, n, jnp.minimum(large, NUM_BUCKETS - 1))


def _bias_table(tab, dist, valid):
    bias = jnp.where(valid[..., None], tab.astype(F32)[_t5_bucket(dist)], NEG_INF)
    bias = jnp.swapaxes(bias, -1, -2)
    return bias.reshape(bias.shape[:-2] + (NSA_H * dist.shape[-1],))


def _nsa_weights(w_in, gate_b, pe, w1, w2, w_out):
    n_att = NSA_H * NSA_DH + 6 * NSA_G * NSA_DH
    eye2 = jnp.eye(2, dtype=F32)
    w1r = w1.reshape(2, NSA_L_CMP, NSA_DH, -1)
    w1bd = jnp.einsum("cldj,pq->clpdqj", w1r, eye2).reshape(2, NSA_L_CMP * 2 * NSA_DH, 2 * w1.shape[-1])
    w2bd = jnp.einsum("cjd,pq->cpjqd", w2, eye2).reshape(2, 2 * w2.shape[1], 2 * NSA_DH)
    pet = jnp.broadcast_to(pe.transpose(0, 2, 1)[:, None, :, None, :],
                           (2, 2, NSA_DH, PAGE // NSA_L_CMP, NSA_L_CMP)).reshape(2, 2 * NSA_DH, PAGE)
    return dict(wt_in=w_in[:, :n_att].T.astype(BF16), w_gate=w_in[:, n_att:].T.astype(BF16),
                gate_b=gate_b.reshape(-1, 1), pet=pet, w1bd=w1bd.astype(BF16), w2bd=w2bd.astype(BF16),
                w_out=w_out.astype(BF16))


_NSA_WIDTHS = (NSA_H * NSA_DH, 4 * NSA_G * NSA_DH, 2 * NSA_G * NSA_DH)


def _pair_matrix(n_cmp, n_slc_p):
    return (jnp.arange(n_cmp)[None, :] // (NSA_L_SLC // NSA_L_CMP) == jnp.arange(n_slc_p)[:, None]).astype(F32)


def _block_expand(n_tiles, n_slc_p):
    et = jnp.arange(n_tiles * ATT_TILE)[:, None] // NSA_L_SLC == jnp.arange(n_slc_p)[None, :]
    return et.astype(BF16).reshape(n_tiles, ATT_TILE, n_slc_p)


def _nsa_prompt(x, seq_len, w, tab):
    nseq = x.shape[0] // seq_len
    tq = tk = ATT_TILE
    nq = seq_len // tq
    n_cmp = seq_len // NSA_L_CMP
    n_slc = -(-seq_len // NSA_L_SLC)
    n_slc_p = -(-n_slc // 8) * 8
    qt, kvt, wint = _projt(x, w["wt_in"], _NSA_WIDTHS, seq_len)
    cmp = _compress(kvt, w["pet"], w["w1bd"], w["w2bd"]).reshape(2, nseq, n_cmp, NSA_G * NSA_DH)

    rq = jnp.arange(tq, dtype=jnp.int32)[None, None, :]
    rk = jnp.arange(tk, dtype=jnp.int32)[None, :, None]
    qpos = (jnp.arange(nq, dtype=jnp.int32) * tq)[:, None, None] + rq
    dist = qpos - (jnp.arange(n_cmp, dtype=jnp.int32) * NSA_L_CMP + NSA_L_CMP - 1)[None, :, None]
    oc, sel = _cmp_attn(qt, cmp[0], cmp[1], _bias_table(tab, dist, dist >= 0), _pair_matrix(n_cmp, n_slc_p), n_slc)

    d_s = jnp.arange(3, dtype=jnp.int32)[:, None, None] * tk + rq - rk
    d_w = jnp.arange(NSA_WINDOW // tk + 1, dtype=jnp.int32)[:, None, None] * tk + rq - rk
    osl, ow = _sw_attn(qt, kvt, wint, sel, _block_expand(seq_len // tk, n_slc_p), _bias_table(tab, d_s, d_s >= 0),
                       _bias_table(tab, d_w, (d_w >= 0) & (d_w < NSA_WINDOW)))
    y = _nsa_out(x, oc, osl, ow, w["w_gate"], w["gate_b"], w["w_out"])
    kv_new = kvt.reshape(nseq, 4, NSA_G, NSA_DH, seq_len).transpose(0, 4, 1, 2, 3)
    win_new = wint.reshape(nseq, 2, NSA_G, NSA_DH, seq_len).transpose(0, 4, 1, 2, 3)
    return y, kv_new, win_new


def _nsa_decode(x, n_tok, q_pos0, cache, win_past, page_table, w, tab):
    n = x.shape[0]
    nseq = n // n_tok
    tq = 8
    lanes = NSA_H * tq
    feats = NSA_G * NSA_DH
    n_pages = page_table.shape[1]
    past = n_pages * PAGE
    n_cmp = (past + n_tok) // NSA_L_CMP
    n_slc = -(-(past + n_tok) // NSA_L_SLC)
    n_slc_p = -(-n_slc // 8) * 8
    cache_t = cache.transpose(0, 2, 3, 4, 1)[None]
    win_t = win_past.transpose(0, 2, 3, 4, 1).reshape(nseq, 2, feats, win_past.shape[1])
    qt, kvt, wint = _projt(x, w["wt_in"], _NSA_WIDTHS, n)

    cmp = _compress(cache_t, w["pet"], w["w1bd"], w["w2bd"], page_table).reshape(2, nseq, n_cmp, feats)

    qg = qt[0].reshape(NSA_G, NSA_HG, NSA_DH, n_tok, nseq).transpose(4, 0, 2, 1, 3)
    qg = jnp.pad(qg, ((0, 0),) * 4 + ((0, tq - n_tok),)).reshape(nseq, NSA_G, NSA_DH, NSA_HG * tq)
    qbd = jnp.einsum("bgdl,gh->bgdhl", qg, jnp.eye(NSA_G, dtype=F32)).reshape(nseq, feats, lanes)
    new_t = lambda a: jnp.pad(a.reshape(feats, n_tok, nseq).transpose(2, 0, 1), ((0, 0), (0, 0), (0, PAGE - n_tok)))
    kv_new = kvt[0].reshape(4, NSA_G, NSA_DH, n_tok, nseq)
    win_new = wint[0].reshape(2, NSA_G, NSA_DH, n_tok, nseq)

    lane = jnp.arange(lanes, dtype=jnp.int32)
    qpos = q_pos0 + (lane % tq)[None, None, :]
    head = (lane // tq)[None, None, :]
    tabf = tab.astype(F32)

    def bias_of(kpos, extra):
        d = qpos - kpos[:, :, None]
        return jnp.where((d >= 0) & extra(d, kpos[:, :, None]), tabf[_t5_bucket(d), head], NEG_INF)

    tile_pos = lambda nt: (jnp.arange(nt, dtype=jnp.int32) * PAGE)[:, None] + jnp.arange(PAGE, dtype=jnp.int32)[None, :]
    kend = (jnp.arange(n_cmp, dtype=jnp.int32) * NSA_L_CMP + NSA_L_CMP - 1)[None, :]
    bias_c = bias_of(kend, lambda d, k: True)[0]
    bsel = bias_of(tile_pos(n_pages + 1), lambda d, k: k < past + n_tok)
    n_wt = win_past.shape[1] // PAGE
    wpos = jnp.concatenate([q_pos0 - n_wt * PAGE + tile_pos(n_wt), q_pos0 + tile_pos(1)])
    bwin = bias_of(wpos, lambda d, k: (d < NSA_WINDOW) & (k >= 0) & (k < q_pos0 + n_tok))
    hsum = ((lane[:, None] // (NSA_HG * tq) == lane[None, :] // (NSA_HG * tq))
            & (lane[:, None] % tq == lane[None, :] % tq)).astype(F32)

    outs = _dec_attn(page_table, cache_t, qbd, cmp[0], cmp[1], bias_c, _pair_matrix(n_cmp, n_slc_p), hsum,
                     new_t(kv_new[2]), new_t(kv_new[3]), _block_expand(n_pages + 1, n_slc_p), bsel, win_t,
                     new_t(win_new[0]), new_t(win_new[1]), bwin, n_slc, q_pos0, tq)

    def own_group(o):
        o = o.reshape(nseq, NSA_G, NSA_DH, NSA_G, NSA_HG, tq)
        o = jnp.diagonal(o, axis1=1, axis2=3)[:, :, :, :n_tok]
        return o.transpose(4, 2, 1, 3, 0).reshape(1, D_MODEL, n)

    y = _nsa_out(x, *(own_group(o) for o in outs), w["w_gate"], w["gate_b"], w["w_out"])
    return y, kv_new.transpose(4, 3, 0, 1, 2), win_new.transpose(4, 3, 0, 1, 2)


def kernel(x_prompt, x_sample, state_gdn_s, state_gdn_conv, cache_nsa_kv, cache_nsa_win, page_table, p_prompt,
           p_sample, ln_g, ln_b, ffn_w_up, ffn_w_down, ple_w_gate, ple_w_proj, a_w_in, a_ln_g, a_ln_b, a_w_s, a_b_s,
           a_w_out, gdn_w_in, gdn_conv_w, gdn_a_log, gdn_dt_bias, gdn_norm_g, gdn_w_out, nsa_w_in, nsa_gate_b,
           nsa_cmp_pe, nsa_cmp_w1, nsa_cmp_w2, nsa_w_out, t5_bias_table):
    bp, tp, _ = x_prompt.shape
    bs, ts, _ = x_sample.shape
    bf = lambda a: a.astype(BF16)
    row = lambda a: a.reshape(1, -1)
    xp = x_prompt.reshape(bp * tp, D_MODEL)
    xs = x_sample.transpose(1, 0, 2).reshape(ts * bs, D_MODEL)
    pp = p_prompt.reshape(DEPTH, bp * tp, -1)
    ps = p_sample.transpose(0, 2, 1, 3).reshape(DEPTH, ts * bs, -1)
    outs = {k: [] for k in ("a_v", "s_p", "c_p", "s_s", "c_s", "kv_p", "win_p", "kv_s", "win_s")}
    ia = ib = ic = 0
    for i in range(DEPTH):
        wu, wd = bf(ffn_w_up[i, 0]), bf(ffn_w_down[i, 0])
        xp = _ffn(xp, wu, wd, row(ln_g[i, 0]), row(ln_b[i, 0]))
        xs = _ffn(xs, wu, wd, row(ln_g[i, 0]), row(ln_b[i, 0]))
        kind = i % 3
        if kind == 0:
            causal = jnp.tril(jnp.ones((A_CHUNK, A_CHUNK), bool))
            ws = bf(jnp.where(causal, a_w_s[ia], 0.0))
            bexp = jnp.repeat(a_b_s[ia].T, 128, axis=1)
            win, wout = bf(a_w_in[ia]), bf(a_w_out[ia])
            lg, lb = row(a_ln_g[ia]), row(a_ln_b[ia])
            (yp,) = _mix_a(xp, win, lg, lb, ws, bexp, wout, decode=False)
            ws_dec = jnp.repeat(a_w_s[ia][:, :ts, :ts].transpose(1, 2, 0).reshape(ts * ts, A_GROUPS), 128, axis=1)
            ys, v_rows = _mix_a(xs, win, lg, lb, ws_dec, bexp[:8], wout, decode=True)
            outs["a_v"].append(v_rows.reshape(ts, bs, A_INNER).transpose(1, 0, 2))
            ia += 1
        elif kind == 1:
            w = gdn_w_in[ib]
            wqkv, wz = bf(w[:, :GDN_QKV]), bf(w[:, GDN_QKV:GDN_QKV + D_MODEL])
            wba = bf(jnp.pad(w[:, GDN_QKV + D_MODEL:], ((0, 0), (0, 128 - 2 * GDN_H))))
            arow = jnp.pad(-jnp.exp(gdn_a_log[ib]), (GDN_H, 128 - 2 * GDN_H)).reshape(1, 128)
            dtrow = jnp.pad(gdn_dt_bias[ib], (GDN_H, 128 - 2 * GDN_H)).reshape(1, 128)
            ng, wout = row(jnp.tile(gdn_norm_g[ib], GDN_H)), bf(gdn_w_out[ib])
            q, k, v, z, ba, tail = _gdn_pre(xp, wqkv, wz, wba, gdn_conv_w[ib], arow, dtrow, seq_len=tp)
            o, s_p = _gdn_scan(q, k, v, ba, jnp.zeros((bp, GDN_H, GDN_D, GDN_D), F32), GDN_CHUNK)
            yp = _gdn_out(o, z, ng, wout)
            outs["s_p"].append(s_p)
            outs["c_p"].append(tail[:, 8 - (GDN_CONV - 1):])
            buf_t = state_gdn_conv[ib].transpose(1, 0, 2).reshape((GDN_CONV - 1) * bs, GDN_QKV)
            q, k, v, z, ba, nbuf = _gdn_pre(xs, wqkv, wz, wba, gdn_conv_w[ib], arow, dtrow, conv_buf=buf_t)
            seq8 = lambda a: jnp.pad(a.reshape(ts, bs, -1).transpose(1, 0, 2), ((0, 0), (0, 8 - ts), (0, 0))).reshape(
                bs * 8, -1)
            o, s_s = _gdn_scan(seq8(q), seq8(k), seq8(v), seq8(ba), state_gdn_s[ib], 8)
            o = o.reshape(bs, 8, D_MODEL)[:, :ts].transpose(1, 0, 2).reshape(ts * bs, D_MODEL)
            ys = _gdn_out(o, z, ng, wout)
            outs["s_s"].append(s_s)
            outs["c_s"].append(nbuf.reshape(GDN_CONV - 1, bs, GDN_QKV).transpose(1, 0, 2))
            ib += 1
        else:
            w = _nsa_weights(nsa_w_in[ic], nsa_gate_b[ic], nsa_cmp_pe[ic], nsa_cmp_w1[ic], nsa_cmp_w2[ic], nsa_w_out[ic])
            yp, kvn_p, winn_p = _nsa_prompt(xp, tp, w, t5_bias_table)
            past_len = page_table.shape[1] * cache_nsa_kv.shape[2]
            ys, kvn_s, winn_s = _nsa_decode(xs, ts, past_len, cache_nsa_kv[ic], cache_nsa_win[ic], page_table, w,
                                            t5_bias_table)
            outs["kv_p"].append(kvn_p)
            outs["win_p"].append(winn_p[:, tp - min(NSA_WINDOW, tp):])
            outs["kv_s"].append(kvn_s)
            outs["win_s"].append(winn_s)
            ic += 1
        wu, wd = bf(ffn_w_up[i, 1]), bf(ffn_w_down[i, 1])
        wg, wp = bf(ple_w_gate[i]), bf(ple_w_proj[i])
        pre = lambda y: (y, row(ln_g[i, 1]), row(ln_b[i, 1]))
        xp = _ffn(xp, wu, wd, row(ln_g[i, 2]), row(ln_b[i, 2]), pre=pre(yp), ple=(pp[i], wg, wp))
        xs = _ffn(xs, wu, wd, row(ln_g[i, 2]), row(ln_b[i, 2]), pre=pre(ys), ple=(ps[i], wg, wp))
    st = lambda k: jnp.stack(outs[k])
    return (xp.reshape(bp, tp, D_MODEL), xs.reshape(ts, bs, D_MODEL).transpose(1, 0, 2), st("a_v"), st("s_p"),
            st("c_p"), st("s_s"), st("c_s"), st("kv_p"), st("win_p"), st("kv_s"), st("win_s"))
```

```python
import functools
import math

import jax
import jax.numpy as jnp
from jax import lax
from jax.experimental import pallas as pl
from jax.experimental.pallas import tpu as pltpu

F32 = jnp.float32
BF16 = jnp.bfloat16
HIGHEST = lax.Precision.HIGHEST

D_MODEL = 1024
DEPTH = 4
ALPHA = (2 * DEPTH) ** 0.25
LN_EPS = 1e-5
NORM_EPS = 1e-6
D_FF = 2816
A_CHUNK = 128
A_INNER = 2 * D_MODEL
A_GROUPS = 16
GDN_H = 8
GDN_D = 128
GDN_QKV = 3 * D_MODEL
GDN_CONV = 4
GDN_CHUNK = 128
NSA_H = 16
NSA_DH = 64
NSA_G = 4
NSA_HG = 4
NSA_L_CMP = 32
NSA_L_SLC = 64
NSA_N_SEL = 16
NSA_WINDOW = 512
NSA_FORCE = 100.0
NEG_INF = -1e30
NUM_BUCKETS = 32
MAX_DISTANCE = 128

ROW_TILE = 256
ATT_TILE = 128
SW_TILE = 256
VMEM_LIMIT = 56 << 20


def _cparams(*sem):
    return pltpu.CompilerParams(dimension_semantics=sem, vmem_limit_bytes=VMEM_LIMIT)


def _resident(shape):
    nd = len(shape)
    return pl.BlockSpec(shape, lambda *_: (0,) * nd, pipeline_mode=pl.Buffered(1))


def _rows(tm, width):
    return pl.BlockSpec((tm, width), lambda i: (i, 0))


def _bdot(a, b):
    return jnp.dot(a.astype(BF16), b.astype(BF16), preferred_element_type=F32)


def _bdot_nt(a, b):
    return lax.dot_general(a.astype(BF16), b.astype(BF16), (((1,), (1,)), ((), ())),
                           preferred_element_type=F32)


def _bdot_tn(a, b):
    return lax.dot_general(a.astype(BF16), b.astype(BF16), (((0,), (0,)), ((), ())),
                           preferred_element_type=F32)


def _hdot(a, b):
    return jnp.dot(a, b, precision=HIGHEST, preferred_element_type=F32)


def _split(a):
    hi = a.astype(BF16)
    return hi, (a - hi.astype(F32)).astype(BF16)


def _dot3(a, b):
    d = lambda x, y: jnp.dot(x, y, preferred_element_type=F32)
    return d(a[0], b[0]) + (d(a[0], b[1]) + d(a[1], b[0]))


def _sigmoid(x):
    return 1.0 / (1.0 + jnp.exp(-x))


def _silu(x):
    return x * _sigmoid(x)


def _gelu(x):
    return 0.5 * x * (1.0 + jnp.tanh(math.sqrt(2.0 / math.pi) * (x + 0.044715 * (x * x * x))))


def _softplus(x):
    return jnp.maximum(x, 0.0) + jnp.log(1.0 + jnp.exp(-jnp.abs(x)))


def _ln(y, g, b):
    mu = jnp.mean(y, axis=-1, keepdims=True)
    d = y - mu
    var = jnp.mean(d * d, axis=-1, keepdims=True)
    return d * lax.rsqrt(var + LN_EPS) * g + b


def _ffn_body(*refs, pre_norm, ple):
    it = iter(refs)
    x_ref = next(it)
    if pre_norm:
        y_ref, g1, b1 = next(it), next(it), next(it)
    wu, wd, g2, b2 = next(it), next(it), next(it), next(it)
    if ple:
        p_ref, wg, wp = next(it), next(it), next(it)
    o_ref = next(it)
    x = x_ref[...]
    if pre_norm:
        x = _ln(ALPHA * x + y_ref[...], g1[...], b1[...])
    h = _bdot(x, wu[...])
    act = _silu(h[:, :D_FF]) * h[:, D_FF:]
    x = _ln(ALPHA * x + 0.5 * _bdot(act, wd[...]), g2[...], b2[...])
    if ple:
        x = x + _sigmoid(_bdot(x, wg[...])) * _bdot(p_ref[...], wp[...])
    o_ref[...] = x


def _ffn(x, wu, wd, g2, b2, pre=None, ple=None):
    n = x.shape[0]
    tm = ROW_TILE
    args, specs = [x], [_rows(tm, D_MODEL)]
    if pre is not None:
        y, g1, b1 = pre
        args += [y, g1, b1]
        specs += [_rows(tm, D_MODEL), _resident((1, D_MODEL)), _resident((1, D_MODEL))]
    args += [wu, wd, g2, b2]
    specs += [_resident(wu.shape), _resident(wd.shape), _resident((1, D_MODEL)), _resident((1, D_MODEL))]
    if ple is not None:
        p, wg, wp = ple
        args += [p, wg, wp]
        specs += [_rows(tm, p.shape[1]), _resident(wg.shape), _resident(wp.shape)]
    return pl.pallas_call(
        functools.partial(_ffn_body, pre_norm=pre is not None, ple=ple is not None),
        grid=(n // tm,), in_specs=specs, out_specs=_rows(tm, D_MODEL),
        out_shape=jax.ShapeDtypeStruct((n, D_MODEL), F32),
        compiler_params=_cparams("parallel"), name="ffn")(*args)


def _mixa_body(x_ref, win, lg, lb, ws, bs, wout, y_ref, *rest, decode):
    if decode:
        v_ref, s_ref = rest
    else:
        (s_ref,) = rest
    tm = x_ref.shape[0]
    h = _gelu(_bdot(x_ref[...], win[...]))
    u = h[:, :A_INNER]
    v = _ln(h[:, A_INNER:], lg[...], lb[...])
    if decode:
        v_ref[...] = v
        nb = tm // 4
        for t in range(4):
            acc = bs[t:t + 1, :]
            for s in range(t + 1):
                acc = acc + ws[4 * t + s:4 * t + s + 1, :] * v[s * nb:(s + 1) * nb, :]
            s_ref[t * nb:(t + 1) * nb, :] = acc
    else:
        for c in range(tm // A_CHUNK):
            r = slice(c * A_CHUNK, (c + 1) * A_CHUNK)
            for g in range(A_GROUPS):
                l = slice(g * 128, (g + 1) * 128)
                s_ref[r, l] = jnp.dot(ws[g], v[r, l].astype(BF16), preferred_element_type=F32) + bs[:, l]
    y_ref[...] = _bdot(u * s_ref[...], wout[...])


def _mix_a(x, win, lg, lb, ws, bs, wout, decode):
    n = x.shape[0]
    tm = n if decode else ROW_TILE
    out_shape = [jax.ShapeDtypeStruct((n, D_MODEL), F32)]
    out_specs = [_rows(tm, D_MODEL)]
    if decode:
        out_shape.append(jax.ShapeDtypeStruct((n, A_INNER), F32))
        out_specs.append(_rows(tm, A_INNER))
    res = pl.pallas_call(
        functools.partial(_mixa_body, decode=decode),
        grid=(n // tm,),
        in_specs=[_rows(tm, D_MODEL), _resident(win.shape), _resident((1, A_INNER)), _resident((1, A_INNER)),
                  _resident(ws.shape), _resident(bs.shape), _resident(wout.shape)],
        out_specs=out_specs, out_shape=out_shape,
        scratch_shapes=[pltpu.VMEM((tm, A_INNER), F32)],
        compiler_params=_cparams("parallel"), name="mix_a_dec" if decode else "mix_a")(x, win, lg, lb, ws, bs, wout)
    return res


def _gdn_post(act, x, wz, wba, arow, dtrow, q_o, k_o, v_o, z_o, ba_o):
    for h in range(GDN_H):
        l = slice(h * GDN_D, (h + 1) * GDN_D)
        qh = act[:, l]
        q_o[:, l] = qh * lax.rsqrt(jnp.sum(qh * qh, axis=-1, keepdims=True) + NORM_EPS) * (GDN_D ** -0.5)
        kh = act[:, D_MODEL + h * GDN_D:D_MODEL + (h + 1) * GDN_D]
        k_o[:, l] = kh * lax.rsqrt(jnp.sum(kh * kh, axis=-1, keepdims=True) + NORM_EPS)
    v_o[...] = act[:, 2 * D_MODEL:]
    z_o[...] = _bdot(x, wz[...])
    lg = _bdot(x, wba[...])
    lane = lax.broadcasted_iota(jnp.int32, lg.shape, 1)
    ba_o[...] = jnp.where(lane < GDN_H, _sigmoid(lg), arow[...] * _softplus(lg + dtrow[...]))


def _gdn_pre_body(x_ref, xh_ref, wqkv, wz, wba, cw, arow, dtrow, q_o, k_o, v_o, z_o, ba_o, tail_o, ext,
                  *, tiles_per_seq):
    tm = x_ref.shape[0]
    x = x_ref[...]
    pre = _bdot(x, wqkv[...])
    halo = _bdot(xh_ref[...], wqkv[...])
    first = pl.program_id(0) % tiles_per_seq == 0
    ext[0:8, :] = jnp.where(first, 0.0, halo)
    ext[8:8 + tm, :] = pre
    tail_o[0] = pre[tm - 8:, :]
    conv = ext[pl.ds(5, tm), :] * cw[0:1, :]
    for j in range(1, GDN_CONV):
        conv = conv + ext[pl.ds(5 + j, tm), :] * cw[j:j + 1, :]
    _gdn_post(_silu(conv), x, wz, wba, arow, dtrow, q_o, k_o, v_o, z_o, ba_o)


def _gdn_pre_dec_body(x_ref, buf_ref, wqkv, wz, wba, cw, arow, dtrow, q_o, k_o, v_o, z_o, ba_o, nbuf_o, ext):
    n = x_ref.shape[0]
    nb = n // 4
    x = x_ref[...]
    ext[0:3 * nb, :] = buf_ref[...]
    ext[3 * nb:3 * nb + n, :] = _bdot(x, wqkv[...])
    nbuf_o[...] = ext[4 * nb:7 * nb, :]
    conv = ext[0:n, :] * cw[0:1, :]
    for j in range(1, GDN_CONV):
        conv = conv + ext[j * nb:j * nb + n, :] * cw[j:j + 1, :]
    _gdn_post(_silu(conv), x, wz, wba, arow, dtrow, q_o, k_o, v_o, z_o, ba_o)


def _gdn_pre(x, wqkv, wz, wba, cw, arow, dtrow, seq_len=None, conv_buf=None):
    n = x.shape[0]
    w_specs = [_resident(wqkv.shape), _resident(wz.shape), _resident(wba.shape), _resident(cw.shape),
               _resident((1, 128)), _resident((1, 128))]
    outs = [jax.ShapeDtypeStruct((n, D_MODEL), F32)] * 4 + [jax.ShapeDtypeStruct((n, 128), F32)]
    if conv_buf is None:
        tm = ROW_TILE
        tps = seq_len // tm
        nseq = n // seq_len
        return pl.pallas_call(
            functools.partial(_gdn_pre_body, tiles_per_seq=tps), grid=(n // tm,),
            in_specs=[_rows(tm, D_MODEL),
                      pl.BlockSpec((8, D_MODEL), lambda i: (jnp.maximum(i * (tm // 8) - 1, 0), 0))] + w_specs,
            out_specs=[_rows(tm, D_MODEL)] * 4 + [_rows(tm, 128),
                                                   pl.BlockSpec((1, 8, GDN_QKV), lambda i: (i // tps, 0, 0))],
            out_shape=outs + [jax.ShapeDtypeStruct((nseq, 8, GDN_QKV), F32)],
            scratch_shapes=[pltpu.VMEM((tm + 8, GDN_QKV), F32)],
            compiler_params=_cparams("arbitrary"), name="gdn_pre")(x, x, wqkv, wz, wba, cw, arow, dtrow)
    nb = n // 4
    return pl.pallas_call(
        _gdn_pre_dec_body, grid=(1,),
        in_specs=[_rows(n, D_MODEL), _rows(3 * nb, GDN_QKV)] + w_specs,
        out_specs=[_rows(n, D_MODEL)] * 4 + [_rows(n, 128), _rows(3 * nb, GDN_QKV)],
        out_shape=outs + [jax.ShapeDtypeStruct((3 * nb, GDN_QKV), F32)],
        scratch_shapes=[pltpu.VMEM((7 * nb, GDN_QKV), F32)],
        compiler_params=_cparams("arbitrary"), name="gdn_pre_dec")(x, conv_buf, wqkv, wz, wba, cw, arow, dtrow)


def _gdn_scan_body(q_ref, k_ref, v_ref, ba_ref, s0_ref, o_ref, s_ref):
    c = q_ref.shape[0]
    heads = range(GDN_H)

    @pl.when(pl.program_id(1) == 0)
    def _():
        s_ref[...] = s0_ref[...]

    ba = ba_ref[...]
    ri = lax.broadcasted_iota(jnp.int32, (c, c), 0)
    ci = lax.broadcasted_iota(jnp.int32, (c, c), 1)
    incl = ci <= ri
    strict = ci < ri
    eye = (ri == ci).astype(F32)
    gcum_all = _hdot(incl.astype(F32), ba)
    lanes = [slice(h * GDN_D, (h + 1) * GDN_D) for h in heads]
    gc, decay, kbeta, inv, pw = [], [], [], [], []
    for h in heads:
        kh = k_ref[:, lanes[h]]
        g = gcum_all[:, GDN_H + h:GDN_H + h + 1]
        g_row = jnp.sum(g * eye, axis=0, keepdims=True)
        d = jnp.where(incl, jnp.exp(jnp.where(incl, g - g_row, 0.0)), 0.0)
        kb = kh * ba[:, h:h + 1]
        a = jnp.where(strict, _bdot_nt(kb, kh) * d, 0.0)
        gc.append(g), decay.append(d), kbeta.append(kb), inv.append(eye - a), pw.append(a)
    pw = [_split(p) for p in pw]
    for _ in range(int(math.log2(c)) - 1):
        pw = [_split(_dot3(p, p)) for p in pw]
        inv = [t + _dot3(_split(t), p) for t, p in zip(inv, pw)]
    u = [_bdot(inv[h], v_ref[:, lanes[h]] * ba[:, h:h + 1]) for h in heads]
    w = [_bdot(inv[h], kbeta[h] * jnp.exp(gc[h])) for h in heads]
    qk = [jnp.where(incl, _bdot_nt(q_ref[:, lanes[h]], k_ref[:, lanes[h]]) * decay[h], 0.0) for h in heads]
    s = [s_ref[0, h] for h in heads]
    v_new = [u[h] - _bdot(w[h], s[h]) for h in heads]
    for h in heads:
        o_ref[:, lanes[h]] = _bdot(q_ref[:, lanes[h]] * jnp.exp(gc[h]), s[h]) + _bdot(qk[h], v_new[h])
    for h in heads:
        g_last = gc[h][c - 1:c, :]
        kd = k_ref[:, lanes[h]] * jnp.exp(g_last - gc[h])
        s_ref[0, h] = s[h] * jnp.exp(g_last) + _bdot_tn(kd, v_new[h])


def _gdn_scan(q, k, v, ba, s0, chunk):
    n = q.shape[0]
    nseq = s0.shape[0]
    nch = n // nseq // chunk
    row = lambda w: pl.BlockSpec((chunk, w), lambda b, c: (b * nch + c, 0))
    st = pl.BlockSpec((1, GDN_H, GDN_D, GDN_D), lambda b, c: (b, 0, 0, 0))
    return pl.pallas_call(
        _gdn_scan_body, grid=(nseq, nch),
        in_specs=[row(D_MODEL), row(D_MODEL), row(D_MODEL), row(128), st],
        out_specs=[row(D_MODEL), st],
        out_shape=[jax.ShapeDtypeStruct((n, D_MODEL), F32), jax.ShapeDtypeStruct(s0.shape, F32)],
        compiler_params=_cparams("parallel", "arbitrary"), name="gdn_scan")(q, k, v, ba, s0)


def _gdn_out_body(o_ref, z_ref, ng, wout, y_ref, t_ref):
    for h in range(GDN_H):
        l = slice(h * GDN_D, (h + 1) * GDN_D)
        oh = o_ref[:, l]
        t_ref[:, l] = oh * lax.rsqrt(jnp.mean(oh * oh, axis=-1, keepdims=True) + NORM_EPS)
    y_ref[...] = _bdot(t_ref[...] * ng[...] * _silu(z_ref[...]), wout[...])


def _gdn_out(o, z, ng, wout):
    n = o.shape[0]
    tm = ROW_TILE
    return pl.pallas_call(
        _gdn_out_body, grid=(n // tm,),
        in_specs=[_rows(tm, D_MODEL), _rows(tm, D_MODEL), _resident((1, D_MODEL)), _resident(wout.shape)],
        out_specs=_rows(tm, D_MODEL), out_shape=jax.ShapeDtypeStruct((n, D_MODEL), F32),
        scratch_shapes=[pltpu.VMEM((tm, D_MODEL), F32)],
        compiler_params=_cparams("parallel"), name="gdn_out")(o, z, ng, wout)


NSA_SCALE = NSA_DH ** -0.5
PAGE = 128


def _projt_body(x_ref, wt_ref, *o_refs):
    yt = _bdot_nt(wt_ref[...], x_ref[...])
    off = 0
    for o in o_refs:
        wdt = o.shape[1]
        o[0] = yt[off:off + wdt, :]
        off += wdt


def _projt(x, wt, widths, seq_len):
    n = x.shape[0]
    tm = ROW_TILE
    tps = seq_len // tm
    return pl.pallas_call(
        _projt_body, grid=(n // tm,),
        in_specs=[_rows(tm, D_MODEL), _resident(wt.shape)],
        out_specs=[pl.BlockSpec((1, wd, tm), lambda i: (i // tps, 0, i % tps)) for wd in widths],
        out_shape=[jax.ShapeDtypeStruct((n // seq_len, wd, seq_len), F32) for wd in widths],
        compiler_params=_cparams("parallel"), name="nsa_proj")(x, wt)


def _cmp_body(*refs, n_pages, group, paged):
    if paged:
        refs = refs[1:]
    srcs, (pet, w1, w2, o_ref, tall, flat) = refs[:-6], refs[-6:]
    n_blk = n_pages * (PAGE // NSA_L_CMP)
    slot = pl.program_id(0) % group
    row0 = pl.multiple_of(slot * n_blk, 8)
    for c in range(2):
        for gp in range(2):
            for p in range(n_pages):
                if paged:
                    slab = srcs[p][0, 0, c, 2 * gp:2 * gp + 2].reshape(2 * NSA_DH, PAGE)
                else:
                    slab = srcs[0][0, (2 * c + gp) * 128:(2 * c + gp + 1) * 128, p * PAGE:(p + 1) * PAGE]
                tall[p * PAGE:(p + 1) * PAGE, :] = (slab + pet[c]).T
            for l in range(NSA_L_CMP):
                flat[2 * c + gp, pl.ds(row0, n_blk), l * 128:(l + 1) * 128] = tall[pl.ds(l, n_blk, stride=NSA_L_CMP), :]

    @pl.when(slot == group - 1)
    def _():
        for c in range(2):
            for gp in range(2):
                hid = _gelu(_bdot(flat[2 * c + gp], w1[c]))
                o_ref[c, :, gp * 128:(gp + 1) * 128] = _bdot(hid, w2[c])


def _compress(src, pet, w1bd, w2bd, page_table=None):
    paged = page_table is not None
    if paged:
        nseq, n_pages = page_table.shape
        group = 4
        src_specs = [pl.BlockSpec((1, 1, 2, NSA_G, NSA_DH, PAGE),
                                  functools.partial(lambda b, pt, p: (0, pt[b, p], 0, 0, 0, 0), p=p))
                     for p in range(n_pages)]
        srcs = [src] * n_pages
        wmap = lambda nd: (lambda b, pt: (0,) * nd)
        omap = lambda b, pt: (0, b // group, 0)
    else:
        nseq, _, t = src.shape
        n_pages = t // PAGE
        group = 1
        src_specs = [pl.BlockSpec((1, 2 * NSA_G * NSA_DH, t), lambda b: (b, 0, 0))]
        srcs = [src]
        wmap = lambda nd: (lambda b: (0,) * nd)
        omap = lambda b: (0, b, 0)
    n_blk = n_pages * (PAGE // NSA_L_CMP)
    in_specs = src_specs + [pl.BlockSpec(a.shape, wmap(a.ndim), pipeline_mode=pl.Buffered(1)) for a in (pet, w1bd, w2bd)]
    out_spec = pl.BlockSpec((2, group * n_blk, NSA_G * NSA_DH), omap)
    scratch = [pltpu.VMEM((n_pages * PAGE, 128), F32), pltpu.VMEM((4, group * n_blk, NSA_L_CMP * 128), F32)]
    out_shape = jax.ShapeDtypeStruct((2, nseq * n_blk, NSA_G * NSA_DH), F32)
    body = functools.partial(_cmp_body, n_pages=n_pages, group=group, paged=paged)
    if paged:
        gs = pltpu.PrefetchScalarGridSpec(num_scalar_prefetch=1, grid=(nseq,), in_specs=in_specs, out_specs=out_spec,
                                          scratch_shapes=scratch)
        return pl.pallas_call(body, grid_spec=gs, out_shape=out_shape, compiler_params=_cparams("arbitrary"),
                              name="nsa_compress_paged")(page_table, *srcs, pet, w1bd, w2bd)
    return pl.pallas_call(body, grid=(nseq,), in_specs=in_specs, out_specs=out_spec, out_shape=out_shape,
                          scratch_shapes=scratch, compiler_params=_cparams("arbitrary"),
                          name="nsa_compress")(*srcs, pet, w1bd, w2bd)


def _select(ps, jq, n_slc):
    n_p, w = ps.shape
    j = lax.broadcasted_iota(jnp.int32, (n_p, w), 0)
    forced = (j == 0) | (j == jq) | (j == jq - 1)
    sc = jnp.where(forced, NSA_FORCE, jnp.where(j > jq, -1.0, ps))
    sc = jnp.where(j >= n_slc, -2.0, sc)
    rank = jnp.zeros((n_p, w), F32)
    for i in range(n_slc):
        r = sc[i:i + 1, :]
        rank = rank + jnp.where((r > sc) | ((r == sc) & (j > i)), 1.0, 0.0)
    return jnp.where(rank < NSA_N_SEL, 1.0, 0.0)


def _softmax_keys(s, mask):
    m = jnp.max(s, axis=0, keepdims=True)
    e = jnp.exp(s - m)
    return jnp.where(mask, e / jnp.sum(e, axis=0, keepdims=True), 0.0)


def _q4(q_ref, g0, tq):
    return jnp.concatenate([q_ref[0, (g0 + hh) * NSA_DH:(g0 + hh + 1) * NSA_DH, :] for hh in range(NSA_HG)], axis=1)


def _cmp_attn_body(q_ref, kc_ref, vc_ref, b_ref, pair_ref, o_ref, sel_ref, *, n_slc):
    tq = q_ref.shape[-1]
    lanes = NSA_HG * tq
    kc, vc = kc_ref[0], vc_ref[0]
    qpos = pl.program_id(1) * tq + lax.broadcasted_iota(jnp.int32, (1, tq), 1)
    for g in range(NSA_G):
        f = slice(g * NSA_DH, (g + 1) * NSA_DH)
        bias = b_ref[0, :, g * lanes:(g + 1) * lanes]
        mask = bias > 0.5 * NEG_INF
        s = jnp.where(mask, _bdot(kc[:, f], _q4(q_ref, g * NSA_HG, tq)) * NSA_SCALE + bias, NEG_INF)
        p = _softmax_keys(s, mask)
        o = _bdot_tn(vc[:, f], p)
        ps = p[:, :tq]
        for hh in range(NSA_HG):
            o_ref[0, (g * NSA_HG + hh) * NSA_DH:(g * NSA_HG + hh + 1) * NSA_DH, :] = o[:, hh * tq:(hh + 1) * tq]
            if hh:
                ps = ps + p[:, hh * tq:(hh + 1) * tq]
        sel_ref[0, g] = _select(_hdot(pair_ref[...], ps), qpos // NSA_L_SLC, n_slc)


def _cmp_attn(qt, kc, vc, bias, pair, n_slc):
    b, _, t = qt.shape
    tq = ATT_TILE
    n_cmp = kc.shape[1]
    n_p = pair.shape[0]
    qs = pl.BlockSpec((1, D_MODEL, tq), lambda b_, i: (b_, 0, i))
    ks = pl.BlockSpec((1, n_cmp, NSA_G * NSA_DH), lambda b_, i: (b_, 0, 0))
    return pl.pallas_call(
        functools.partial(_cmp_attn_body, n_slc=n_slc), grid=(b, t // tq),
        in_specs=[qs, ks, ks, pl.BlockSpec((1, n_cmp, NSA_H * tq), lambda b_, i: (i, 0, 0)),
                  pl.BlockSpec(pair.shape, lambda b_, i: (0, 0))],
        out_specs=[qs, pl.BlockSpec((1, NSA_G, n_p, tq), lambda b_, i: (b_, 0, 0, i))],
        out_shape=[jax.ShapeDtypeStruct(qt.shape, F32), jax.ShapeDtypeStruct((b, NSA_G, n_p, t), F32)],
        compiler_params=_cparams("parallel", "parallel"), name="nsa_cmp_attn")(qt, kc, vc, bias, pair)


def _flash(q, kt, vt, bias, sel_mask, m, l, acc):
    mask = bias > 0.5 * NEG_INF
    if sel_mask is not None:
        mask = mask & sel_mask
    s = jnp.where(mask, _bdot_tn(kt, q) * NSA_SCALE + bias, NEG_INF)
    m_new = jnp.maximum(m, jnp.max(s, axis=0, keepdims=True))
    alpha = jnp.exp(m - m_new)
    p = jnp.where(mask, jnp.exp(s - m_new), 0.0)
    return m_new, alpha * l + jnp.sum(p, axis=0, keepdims=True), alpha * acc + _bdot(vt, p)


def _flash_init(rows, lanes):
    return jnp.full((1, lanes), NEG_INF, F32), jnp.zeros((1, lanes), F32), jnp.zeros((rows, lanes), F32)


def _flash_out(l, acc):
    return acc / jnp.where(l > 0.0, l, 1.0)


def _sw_attn_body(q_ref, ks_ref, vs_ref, sel_ref, b_ref, far_ref, kw_ref, vw_ref, os_ref, ow_ref, sel4, m_s, l_s, acc_s):
    tq = tk = SW_TILE
    lanes = NSA_HG * tq
    blocks = tk // NSA_L_SLC
    qt = pl.program_id(2)
    q = (_q4(q_ref, 0, tq) * NSA_SCALE).astype(BF16)
    sel4[...] = jnp.concatenate([sel_ref[0, 0]] * NSA_HG, axis=1)

    def keys(kt):
        return pl.ds(pl.multiple_of(kt * tk, tk), tk)

    def sel_mask(kt):
        rows = [jnp.broadcast_to(sel4[pl.ds(kt * blocks + i, 1), :], (NSA_L_SLC, lanes)) for i in range(blocks)]
        return jnp.concatenate(rows, axis=0) > 0.5

    def update(s, vt, p_of):
        m_old = m_s[...]
        m_new = jnp.maximum(m_old, jnp.max(s, axis=0, keepdims=True))
        alpha = jnp.exp(m_old - m_new)
        p = p_of(m_new)
        l_s[...] = alpha * l_s[...] + jnp.sum(p, axis=0, keepdims=True)
        acc_s[...] = alpha * acc_s[...] + _bdot(vt, p)
        m_s[...] = m_new

    def near(k_ref, v_ref, kt, bias, smask):
        mask = bias > 0.5 * NEG_INF
        if smask is not None:
            mask = mask & smask
        s = jnp.where(mask, _bdot_tn(k_ref[0, :, keys(kt)], q) + bias, NEG_INF)
        update(s, v_ref[0, :, keys(kt)], lambda m_new: jnp.where(mask, jnp.exp(s - m_new), 0.0))

    def far(kt):
        s = jnp.where(sel_mask(kt), _bdot_tn(ks_ref[0, :, keys(kt)], q) + far_ref[...], NEG_INF)
        update(s, vs_ref[0, :, keys(kt)], lambda m_new: jnp.exp(s - m_new))

    def init():
        m_s[...] = jnp.full(m_s.shape, NEG_INF, F32)
        l_s[...] = jnp.zeros(l_s.shape, F32)
        acc_s[...] = jnp.zeros(acc_s.shape, F32)

    def write(o_ref):
        o = _flash_out(l_s[...], acc_s[...])
        for hh in range(NSA_HG):
            o_ref[0, hh * NSA_DH:(hh + 1) * NSA_DH, :] = o[:, hh * tq:(hh + 1) * tq]

    init()
    near(ks_ref, vs_ref, qt, b_ref[0], sel_mask(qt))
    pl.when(qt >= 1)(lambda: near(ks_ref, vs_ref, qt - 1, b_ref[1], sel_mask(qt - 1)))
    n_far = jnp.maximum(qt - 1, 0)

    def far_pair(i, carry):
        far(2 * i)
        far(2 * i + 1)
        return carry

    lax.fori_loop(0, n_far // 2, far_pair, 0)
    pl.when(n_far % 2 == 1)(lambda: far(n_far - 1))
    write(os_ref)

    init()
    near(kw_ref, vw_ref, qt, b_ref[0], None)
    for jw in range(1, b_ref.shape[0]):
        pl.when(qt >= jw)(functools.partial(near, kw_ref, vw_ref, qt - jw, b_ref[jw], None))
    write(ow_ref)


def _sw_attn(qt, kvt, wint, sel, btab, bfar):
    b, _, t = qt.shape
    tq = tk = SW_TILE
    lanes = NSA_HG * tq
    qs = pl.BlockSpec((1, NSA_HG * NSA_DH, tq), lambda b_, g_, i: (b_, g_, i))
    feat = lambda blk0: pl.BlockSpec((1, NSA_DH, t), lambda b_, g_, i: (b_, blk0 + g_, 0))
    return pl.pallas_call(
        _sw_attn_body, grid=(b, NSA_G, t // tq),
        in_specs=[qs, feat(2 * NSA_G), feat(3 * NSA_G),
                  pl.BlockSpec((1, 1, sel.shape[2], tq), lambda b_, g_, i: (b_, g_, 0, i)),
                  pl.BlockSpec((btab.shape[0], tk, lanes), lambda b_, g_, i: (0, 0, g_)),
                  pl.BlockSpec((1, lanes), lambda b_, g_, i: (0, g_)),
                  feat(0), feat(NSA_G)],
        out_specs=[qs, qs],
        out_shape=[jax.ShapeDtypeStruct(qt.shape, F32)] * 2,
        scratch_shapes=[pltpu.VMEM((sel.shape[2], lanes), F32), pltpu.VMEM((1, lanes), F32),
                        pltpu.VMEM((1, lanes), F32), pltpu.VMEM((NSA_DH, lanes), F32)],
        compiler_params=_cparams("parallel", "parallel", "arbitrary"),
        name="nsa_sw_attn")(qt, kvt, kvt, sel, btab, bfar, wint, wint)


def _dec_attn_body(pt_ref, *refs, n_pages, n_slc, q_pos0, tq):
    pages = refs[:n_pages]
    (q_ref, kc_ref, vc_ref, bc_ref, pair_ref, hsum_ref, knew_ref, vnew_ref, et_ref, bs_ref, win_ref, kwnew_ref,
     vwnew_ref, bw_ref, oc_ref, os_ref, ow_ref) = refs[n_pages:]
    feats, lanes = q_ref.shape[1:]
    q = q_ref[0].astype(BF16)

    bias = bc_ref[...]
    mask = bias > 0.5 * NEG_INF
    p = _softmax_keys(jnp.where(mask, _bdot(kc_ref[0], q) * NSA_SCALE + bias, NEG_INF), mask)
    oc_ref[0] = _bdot_tn(vc_ref[0], p)
    ps = _hdot(pair_ref[...], _hdot(p, hsum_ref[...]))
    lane = lax.broadcasted_iota(jnp.int32, (1, lanes), 1)
    sel = _select(ps, (q_pos0 + lane % tq) // NSA_L_SLC, n_slc).astype(BF16)

    state = _flash_init(feats, lanes)
    for j in range(n_pages + 1):
        if j < n_pages:
            kt, vt = pages[j][0, 0, 0].reshape(feats, PAGE), pages[j][0, 0, 1].reshape(feats, PAGE)
        else:
            kt, vt = knew_ref[0], vnew_ref[0]
        sel_mask = jnp.dot(et_ref[j], sel, preferred_element_type=F32) > 0.5
        state = _flash(q, kt, vt, bs_ref[j], sel_mask, *state)
    os_ref[0] = _flash_out(*state[1:])

    state = _flash_init(feats, lanes)
    n_wt = win_ref.shape[-1] // PAGE
    for j in range(n_wt + 1):
        if j < n_wt:
            kt, vt = win_ref[0, 0, :, j * PAGE:(j + 1) * PAGE], win_ref[0, 1, :, j * PAGE:(j + 1) * PAGE]
        else:
            kt, vt = kwnew_ref[0], vwnew_ref[0]
        state = _flash(q, kt, vt, bw_ref[j], None, *state)
    ow_ref[0] = _flash_out(*state[1:])


def _dec_attn(page_table, cache_t, qbd, kc, vc, bias_c, pair, hsum, knew, vnew, et, bsel, win_t, kwnew, vwnew, bwin,
              n_slc, q_pos0, tq):
    nseq, n_pages = page_table.shape
    per_seq = lambda a: pl.BlockSpec((1,) + a.shape[1:], lambda b, pt: (b,) + (0,) * (a.ndim - 1))
    whole = lambda a: pl.BlockSpec(a.shape, lambda b, pt: (0,) * a.ndim, pipeline_mode=pl.Buffered(1))
    page_specs = [pl.BlockSpec((1, 1, 2, NSA_G, NSA_DH, PAGE),
                               functools.partial(lambda b, pt, p: (0, pt[b, p], 1, 0, 0, 0), p=p))
                  for p in range(n_pages)]
    gs = pltpu.PrefetchScalarGridSpec(
        num_scalar_prefetch=1, grid=(nseq,),
        in_specs=page_specs + [per_seq(qbd), per_seq(kc), per_seq(vc), whole(bias_c), whole(pair), whole(hsum),
                               per_seq(knew), per_seq(vnew), whole(et), whole(bsel), per_seq(win_t), per_seq(kwnew),
                               per_seq(vwnew), whole(bwin)],
        out_specs=[per_seq(qbd)] * 3)
    return pl.pallas_call(
        functools.partial(_dec_attn_body, n_pages=n_pages, n_slc=n_slc, q_pos0=q_pos0, tq=tq), grid_spec=gs,
        out_shape=[jax.ShapeDtypeStruct(qbd.shape, F32)] * 3, compiler_params=_cparams("parallel"),
        name="nsa_dec_attn")(page_table, *([cache_t] * n_pages), qbd, kc, vc, bias_c, pair, hsum, knew, vnew, et, bsel,
                             win_t, kwnew, vwnew, bwin)


def _nsa_out_body(x_ref, oc_ref, os_ref, ow_ref, wg, gb, wout, y_ref):
    gates = _sigmoid(_bdot_nt(wg[...], x_ref[...]) + gb[...])
    parts = []
    for h in range(NSA_H):
        r = slice(h * NSA_DH, (h + 1) * NSA_DH)
        parts.append(gates[h:h + 1, :] * oc_ref[0, r, :] + gates[NSA_H + h:NSA_H + h + 1, :] * os_ref[0, r, :]
                     + gates[2 * NSA_H + h:2 * NSA_H + h + 1, :] * ow_ref[0, r, :])
    y_ref[...] = _bdot_tn(jnp.concatenate(parts, axis=0), wout[...])


def _nsa_out(x, oc, osl, ow, wg, gb, wout):
    n = x.shape[0]
    tm = ROW_TILE
    tps = oc.shape[2] // tm
    col = pl.BlockSpec((1, D_MODEL, tm), lambda i: (i // tps, 0, i % tps))
    return pl.pallas_call(
        _nsa_out_body, grid=(n // tm,),
        in_specs=[_rows(tm, D_MODEL), col, col, col, _resident(wg.shape), _resident(gb.shape), _resident(wout.shape)],
        out_specs=_rows(tm, D_MODEL), out_shape=jax.ShapeDtypeStruct((n, D_MODEL), F32),
        compiler_params=_cparams("parallel"), name="nsa_out")(x, oc, osl, ow, wg, gb, wout)


def _t5_bucket(dist):
    n = jnp.maximum(dist, 0)
    max_exact = NUM_BUCKETS // 2
    nf = jnp.maximum(n, 1).astype(F32)
    large = max_exact + (jnp.log(nf / max_exact) / math.log(MAX_DISTANCE / max_exact)
                         * (NUM_BUCKETS - max_exact)).astype(jnp.int32)
    return jnp.where(n < max_exact, n, jnp.minimum(large, NUM_BUCKETS - 1))


def _bias_table(tab, dist, valid):
    onehot = (_t5_bucket(dist)[..., None] == jnp.arange(NUM_BUCKETS)).astype(F32)
    bias = jnp.einsum("...k,kh->...h", onehot, tab.astype(F32), precision=HIGHEST)
    bias = jnp.swapaxes(jnp.where(valid[..., None], bias, NEG_INF), -1, -2)
    return bias.reshape(bias.shape[:-2] + (NSA_H * dist.shape[-1],))


def _nsa_weights(w_in, gate_b, pe, w1, w2, w_out):
    n_att = NSA_H * NSA_DH + 6 * NSA_G * NSA_DH
    eye2 = jnp.eye(2, dtype=F32)
    w1r = w1.reshape(2, NSA_L_CMP, NSA_DH, -1)
    w1bd = jnp.einsum("cldj,pq->clpdqj", w1r, eye2).reshape(2, NSA_L_CMP * 2 * NSA_DH, 2 * w1.shape[-1])
    w2bd = jnp.einsum("cjd,pq->cpjqd", w2, eye2).reshape(2, 2 * w2.shape[1], 2 * NSA_DH)
    pet = jnp.broadcast_to(pe.transpose(0, 2, 1)[:, None, :, None, :],
                           (2, 2, NSA_DH, PAGE // NSA_L_CMP, NSA_L_CMP)).reshape(2, 2 * NSA_DH, PAGE)
    return dict(wt_in=w_in[:, :n_att].T.astype(BF16), w_gate=w_in[:, n_att:].T.astype(BF16),
                gate_b=gate_b.reshape(-1, 1), pet=pet, w1bd=w1bd.astype(BF16), w2bd=w2bd.astype(BF16),
                w_out=w_out.astype(BF16))


_NSA_WIDTHS = (NSA_H * NSA_DH, 4 * NSA_G * NSA_DH, 2 * NSA_G * NSA_DH)


def _pair_matrix(n_cmp, n_slc_p):
    return (jnp.arange(n_cmp)[None, :] // (NSA_L_SLC // NSA_L_CMP) == jnp.arange(n_slc_p)[:, None]).astype(F32)


def _block_expand(n_tiles, n_slc_p):
    et = jnp.arange(n_tiles * ATT_TILE)[:, None] // NSA_L_SLC == jnp.arange(n_slc_p)[None, :]
    return et.astype(BF16).reshape(n_tiles, ATT_TILE, n_slc_p)


def _nsa_prompt(x, seq_len, w, tab):
    nseq = x.shape[0] // seq_len
    tq = ATT_TILE
    nq = seq_len // tq
    n_cmp = seq_len // NSA_L_CMP
    n_slc = -(-seq_len // NSA_L_SLC)
    n_slc_p = -(-n_slc // 8) * 8
    qt, kvt, wint = _projt(x, w["wt_in"], _NSA_WIDTHS, seq_len)
    cmp = _compress(kvt, w["pet"], w["w1bd"], w["w2bd"]).reshape(2, nseq, n_cmp, NSA_G * NSA_DH)

    qpos = (jnp.arange(nq, dtype=jnp.int32) * tq)[:, None, None] + jnp.arange(tq, dtype=jnp.int32)[None, None, :]
    dist = qpos - (jnp.arange(n_cmp, dtype=jnp.int32) * NSA_L_CMP + NSA_L_CMP - 1)[None, :, None]
    oc, sel = _cmp_attn(qt, cmp[0], cmp[1], _bias_table(tab, dist, dist >= 0), _pair_matrix(n_cmp, n_slc_p), n_slc)

    rs = jnp.arange(SW_TILE, dtype=jnp.int32)
    d_t = (jnp.arange(NSA_WINDOW // SW_TILE + 1, dtype=jnp.int32)[:, None, None] * SW_TILE
           + rs[None, None, :] - rs[None, :, None])
    btab = _bias_table(tab, d_t, (d_t >= 0) & (d_t < NSA_WINDOW))
    bfar = _bias_table(tab, jnp.full((1, SW_TILE), 4 * MAX_DISTANCE, jnp.int32), jnp.ones((1, SW_TILE), bool))
    osl, ow = _sw_attn(qt, kvt, wint, sel, btab, bfar)
    y = _nsa_out(x, oc, osl, ow, w["w_gate"], w["gate_b"], w["w_out"])
    kv_new = kvt.reshape(nseq, 4, NSA_G, NSA_DH, seq_len).transpose(0, 4, 1, 2, 3)
    win_new = wint.reshape(nseq, 2, NSA_G, NSA_DH, seq_len).transpose(0, 4, 1, 2, 3)
    return y, kv_new, win_new


def _nsa_decode(x, n_tok, q_pos0, cache, win_past, page_table, w, tab):
    n = x.shape[0]
    nseq = n // n_tok
    tq = 8
    lanes = NSA_H * tq
    feats = NSA_G * NSA_DH
    n_pages = page_table.shape[1]
    past = n_pages * PAGE
    n_cmp = (past + n_tok) // NSA_L_CMP
    n_slc = -(-(past + n_tok) // NSA_L_SLC)
    n_slc_p = -(-n_slc // 8) * 8
    cache_t = cache.transpose(0, 2, 3, 4, 1)[None]
    win_t = win_past.transpose(0, 2, 3, 4, 1).reshape(nseq, 2, feats, win_past.shape[1])
    qt, kvt, wint = _projt(x, w["wt_in"], _NSA_WIDTHS, n)

    cmp = _compress(cache_t, w["pet"], w["w1bd"], w["w2bd"], page_table).reshape(2, nseq, n_cmp, feats)

    qg = qt[0].reshape(NSA_G, NSA_HG, NSA_DH, n_tok, nseq).transpose(4, 0, 2, 1, 3)
    qg = jnp.pad(qg, ((0, 0),) * 4 + ((0, tq - n_tok),)).reshape(nseq, NSA_G, NSA_DH, NSA_HG * tq)
    qbd = jnp.einsum("bgdl,gh->bgdhl", qg, jnp.eye(NSA_G, dtype=F32)).reshape(nseq, feats, lanes)
    new_t = lambda a: jnp.pad(a.reshape(feats, n_tok, nseq).transpose(2, 0, 1), ((0, 0), (0, 0), (0, PAGE - n_tok)))
    kv_new = kvt[0].reshape(4, NSA_G, NSA_DH, n_tok, nseq)
    win_new = wint[0].reshape(2, NSA_G, NSA_DH, n_tok, nseq)

    lane = jnp.arange(lanes, dtype=jnp.int32)
    qpos = q_pos0 + jnp.arange(tq, dtype=jnp.int32)[None, None, :]

    def bias_of(kpos, extra):
        d = qpos - kpos[:, :, None]
        return _bias_table(tab, d, (d >= 0) & extra(d, kpos[:, :, None]))

    tile_pos = lambda nt: (jnp.arange(nt, dtype=jnp.int32) * PAGE)[:, None] + jnp.arange(PAGE, dtype=jnp.int32)[None, :]
    kend = (jnp.arange(n_cmp, dtype=jnp.int32) * NSA_L_CMP + NSA_L_CMP - 1)[None, :]
    bias_c = bias_of(kend, lambda d, k: True)[0]
    bsel = bias_of(tile_pos(n_pages + 1), lambda d, k: k < past + n_tok)
    n_wt = win_past.shape[1] // PAGE
    wpos = jnp.concatenate([q_pos0 - n_wt * PAGE + tile_pos(n_wt), q_pos0 + tile_pos(1)])
    bwin = bias_of(wpos, lambda d, k: (d < NSA_WINDOW) & (k >= 0) & (k < q_pos0 + n_tok))
    hsum = ((lane[:, None] // (NSA_HG * tq) == lane[None, :] // (NSA_HG * tq))
            & (lane[:, None] % tq == lane[None, :] % tq)).astype(F32)

    outs = _dec_attn(page_table, cache_t, qbd, cmp[0], cmp[1], bias_c, _pair_matrix(n_cmp, n_slc_p), hsum,
                     new_t(kv_new[2]), new_t(kv_new[3]), _block_expand(n_pages + 1, n_slc_p), bsel, win_t,
                     new_t(win_new[0]), new_t(win_new[1]), bwin, n_slc, q_pos0, tq)

    def own_group(o):
        o = o.reshape(nseq, NSA_G, NSA_DH, NSA_G, NSA_HG, tq)
        o = jnp.diagonal(o, axis1=1, axis2=3)[:, :, :, :n_tok]
        return o.transpose(4, 2, 1, 3, 0).reshape(1, D_MODEL, n)

    y = _nsa_out(x, *(own_group(o) for o in outs), w["w_gate"], w["gate_b"], w["w_out"])
    return y, kv_new.transpose(4, 3, 0, 1, 2), win_new.transpose(4, 3, 0, 1, 2)


def kernel(x_prompt, x_sample, state_gdn_s, state_gdn_conv, cache_nsa_kv, cache_nsa_win, page_table, p_prompt,
           p_sample, ln_g, ln_b, ffn_w_up, ffn_w_down, ple_w_gate, ple_w_proj, a_w_in, a_ln_g, a_ln_b, a_w_s, a_b_s,
           a_w_out, gdn_w_in, gdn_conv_w, gdn_a_log, gdn_dt_bias, gdn_norm_g, gdn_w_out, nsa_w_in, nsa_gate_b,
           nsa_cmp_pe, nsa_cmp_w1, nsa_cmp_w2, nsa_w_out, t5_bias_table):
    bp, tp, _ = x_prompt.shape
    bs, ts, _ = x_sample.shape
    bf = lambda a: a.astype(BF16)
    row = lambda a: a.reshape(1, -1)
    xp = x_prompt.reshape(bp * tp, D_MODEL)
    xs = x_sample.transpose(1, 0, 2).reshape(ts * bs, D_MODEL)
    pp = p_prompt.reshape(DEPTH, bp * tp, -1)
    ps = p_sample.transpose(0, 2, 1, 3).reshape(DEPTH, ts * bs, -1)
    outs = {k: [] for k in ("a_v", "s_p", "c_p", "s_s", "c_s", "kv_p", "win_p", "kv_s", "win_s")}
    ia = ib = ic = 0
    for i in range(DEPTH):
        wu, wd = bf(ffn_w_up[i, 0]), bf(ffn_w_down[i, 0])
        xp = _ffn(xp, wu, wd, row(ln_g[i, 0]), row(ln_b[i, 0]))
        xs = _ffn(xs, wu, wd, row(ln_g[i, 0]), row(ln_b[i, 0]))
        kind = i % 3
        if kind == 0:
            causal = jnp.tril(jnp.ones((A_CHUNK, A_CHUNK), bool))
            ws = bf(jnp.where(causal, a_w_s[ia], 0.0))
            bexp = jnp.repeat(a_b_s[ia].T, 128, axis=1)
            win, wout = bf(a_w_in[ia]), bf(a_w_out[ia])
            lg, lb = row(a_ln_g[ia]), row(a_ln_b[ia])
            (yp,) = _mix_a(xp, win, lg, lb, ws, bexp, wout, decode=False)
            ws_dec = jnp.repeat(a_w_s[ia][:, :ts, :ts].transpose(1, 2, 0).reshape(ts * ts, A_GROUPS), 128, axis=1)
            ys, v_rows = _mix_a(xs, win, lg, lb, ws_dec, bexp[:8], wout, decode=True)
            outs["a_v"].append(v_rows.reshape(ts, bs, A_INNER).transpose(1, 0, 2))
            ia += 1
        elif kind == 1:
            w = gdn_w_in[ib]
            wqkv, wz = bf(w[:, :GDN_QKV]), bf(w[:, GDN_QKV:GDN_QKV + D_MODEL])
            wba = bf(jnp.pad(w[:, GDN_QKV + D_MODEL:], ((0, 0), (0, 128 - 2 * GDN_H))))
            arow = jnp.pad(-jnp.exp(gdn_a_log[ib]), (GDN_H, 128 - 2 * GDN_H)).reshape(1, 128)
            dtrow = jnp.pad(gdn_dt_bias[ib], (GDN_H, 128 - 2 * GDN_H)).reshape(1, 128)
            ng, wout = row(jnp.tile(gdn_norm_g[ib], GDN_H)), bf(gdn_w_out[ib])
            q, k, v, z, ba, tail = _gdn_pre(xp, wqkv, wz, wba, gdn_conv_w[ib], arow, dtrow, seq_len=tp)
            o, s_p = _gdn_scan(q, k, v, ba, jnp.zeros((bp, GDN_H, GDN_D, GDN_D), F32), GDN_CHUNK)
            yp = _gdn_out(o, z, ng, wout)
            outs["s_p"].append(s_p)
            outs["c_p"].append(tail[:, 8 - (GDN_CONV - 1):])
            buf_t = state_gdn_conv[ib].transpose(1, 0, 2).reshape((GDN_CONV - 1) * bs, GDN_QKV)
            q, k, v, z, ba, nbuf = _gdn_pre(xs, wqkv, wz, wba, gdn_conv_w[ib], arow, dtrow, conv_buf=buf_t)
            seq8 = lambda a: jnp.pad(a.reshape(ts, bs, -1).transpose(1, 0, 2), ((0, 0), (0, 8 - ts), (0, 0))).reshape(
                bs * 8, -1)
            o, s_s = _gdn_scan(seq8(q), seq8(k), seq8(v), seq8(ba), state_gdn_s[ib], 8)
            o = o.reshape(bs, 8, D_MODEL)[:, :ts].transpose(1, 0, 2).reshape(ts * bs, D_MODEL)
            ys = _gdn_out(o, z, ng, wout)
            outs["s_s"].append(s_s)
            outs["c_s"].append(nbuf.reshape(GDN_CONV - 1, bs, GDN_QKV).transpose(1, 0, 2))
            ib += 1
        else:
            w = _nsa_weights(nsa_w_in[ic], nsa_gate_b[ic], nsa_cmp_pe[ic], nsa_cmp_w1[ic], nsa_cmp_w2[ic], nsa_w_out[ic])
            yp, kvn_p, winn_p = _nsa_prompt(xp, tp, w, t5_bias_table)
            past_len = page_table.shape[1] * cache_nsa_kv.shape[2]
            ys, kvn_s, winn_s = _nsa_decode(xs, ts, past_len, cache_nsa_kv[ic], cache_nsa_win[ic], page_table, w,
                                            t5_bias_table)
            outs["kv_p"].append(kvn_p)
            outs["win_p"].append(winn_p[:, tp - min(NSA_WINDOW, tp):])
            outs["kv_s"].append(kvn_s)
            outs["win_s"].append(winn_s)
            ic += 1
        wu, wd = bf(ffn_w_up[i, 1]), bf(ffn_w_down[i, 1])
        wg, wp = bf(ple_w_gate[i]), bf(ple_w_proj[i])
        pre = lambda y: (y, row(ln_g[i, 1]), row(ln_b[i, 1]))
        xp = _ffn(xp, wu, wd, row(ln_g[i, 2]), row(ln_b[i, 2]), pre=pre(yp), ple=(pp[i], wg, wp))
        xs = _ffn(xs, wu, wd, row(ln_g[i, 2]), row(ln_b[i, 2]), pre=pre(ys), ple=(ps[i], wg, wp))
    st = lambda k: jnp.stack(outs[k])
    return (xp.reshape(bp, tp, D_MODEL), xs.reshape(ts, bs, D_MODEL).transpose(1, 0, 2), st("a_v"), st("s_p"),
            st("c_p"), st("s_s"), st("c_s"), st("kv_p"), st("win_p"), st("kv_s"), st("win_s"))
```

```python
import functools
import math

import jax
import jax.numpy as jnp
from jax import lax
from jax.experimental import pallas as pl
from jax.experimental.pallas import tpu as pltpu

F32 = jnp.float32
BF16 = jnp.bfloat16
HIGHEST = lax.Precision.HIGHEST

D_MODEL = 1024
DEPTH = 4
ALPHA = (2 * DEPTH) ** 0.25
LN_EPS = 1e-5
NORM_EPS = 1e-6
D_FF = 2816
A_CHUNK = 128
A_INNER = 2 * D_MODEL
A_GROUPS = 16
GDN_H = 8
GDN_D = 128
GDN_QKV = 3 * D_MODEL
GDN_CONV = 4
GDN_CHUNK = 128
NSA_H = 16
NSA_DH = 64
NSA_G = 4
NSA_HG = 4
NSA_L_CMP = 32
NSA_L_SLC = 64
NSA_N_SEL = 16
NSA_WINDOW = 512
NSA_FORCE = 100.0
NEG_INF = -1e30
NUM_BUCKETS = 32
MAX_DISTANCE = 128

ROW_TILE = 256
ATT_TILE = 128
SW_TILE = 256
FAR_TILE = 512
VMEM_LIMIT = 56 << 20


def _cparams(*sem):
    return pltpu.CompilerParams(dimension_semantics=sem, vmem_limit_bytes=VMEM_LIMIT)


def _resident(shape):
    nd = len(shape)
    return pl.BlockSpec(shape, lambda *_: (0,) * nd, pipeline_mode=pl.Buffered(1))


def _rows(tm, width):
    return pl.BlockSpec((tm, width), lambda i: (i, 0))


def _bdot(a, b):
    return jnp.dot(a.astype(BF16), b.astype(BF16), preferred_element_type=F32)


def _bdot_nt(a, b):
    return lax.dot_general(a.astype(BF16), b.astype(BF16), (((1,), (1,)), ((), ())),
                           preferred_element_type=F32)


def _bdot_tn(a, b):
    return lax.dot_general(a.astype(BF16), b.astype(BF16), (((0,), (0,)), ((), ())),
                           preferred_element_type=F32)


def _hdot(a, b):
    return jnp.dot(a, b, precision=HIGHEST, preferred_element_type=F32)


def _split(a):
    hi = a.astype(BF16)
    return hi, (a - hi.astype(F32)).astype(BF16)


def _dot3(a, b):
    d = lambda x, y: jnp.dot(x, y, preferred_element_type=F32)
    return d(a[0], b[0]) + (d(a[0], b[1]) + d(a[1], b[0]))


def _sigmoid(x):
    return 1.0 / (1.0 + jnp.exp(-x))


def _silu(x):
    return x * _sigmoid(x)


def _gelu(x):
    return 0.5 * x * (1.0 + jnp.tanh(math.sqrt(2.0 / math.pi) * (x + 0.044715 * (x * x * x))))


def _softplus(x):
    return jnp.maximum(x, 0.0) + jnp.log(1.0 + jnp.exp(-jnp.abs(x)))


def _ln(y, g, b):
    mu = jnp.mean(y, axis=-1, keepdims=True)
    d = y - mu
    var = jnp.mean(d * d, axis=-1, keepdims=True)
    return d * lax.rsqrt(var + LN_EPS) * g + b


def _ffn_body(*refs, pre_norm, ple):
    it = iter(refs)
    x_ref = next(it)
    if pre_norm:
        y_ref, g1, b1 = next(it), next(it), next(it)
    wu, wd, g2, b2 = next(it), next(it), next(it), next(it)
    if ple:
        p_ref, wg, wp = next(it), next(it), next(it)
    o_ref = next(it)
    x = x_ref[...]
    if pre_norm:
        x = _ln(ALPHA * x + y_ref[...], g1[...], b1[...])
    h = _bdot(x, wu[...])
    act = _silu(h[:, :D_FF]) * h[:, D_FF:]
    x = _ln(ALPHA * x + 0.5 * _bdot(act, wd[...]), g2[...], b2[...])
    if ple:
        x = x + _sigmoid(_bdot(x, wg[...])) * _bdot(p_ref[...], wp[...])
    o_ref[...] = x


def _ffn(x, wu, wd, g2, b2, pre=None, ple=None):
    n = x.shape[0]
    tm = ROW_TILE
    args, specs = [x], [_rows(tm, D_MODEL)]
    if pre is not None:
        y, g1, b1 = pre
        args += [y, g1, b1]
        specs += [_rows(tm, D_MODEL), _resident((1, D_MODEL)), _resident((1, D_MODEL))]
    args += [wu, wd, g2, b2]
    specs += [_resident(wu.shape), _resident(wd.shape), _resident((1, D_MODEL)), _resident((1, D_MODEL))]
    if ple is not None:
        p, wg, wp = ple
        args += [p, wg, wp]
        specs += [_rows(tm, p.shape[1]), _resident(wg.shape), _resident(wp.shape)]
    return pl.pallas_call(
        functools.partial(_ffn_body, pre_norm=pre is not None, ple=ple is not None),
        grid=(n // tm,), in_specs=specs, out_specs=_rows(tm, D_MODEL),
        out_shape=jax.ShapeDtypeStruct((n, D_MODEL), F32),
        compiler_params=_cparams("parallel"), name="ffn")(*args)


def _mixa_body(x_ref, win, lg, lb, ws, bs, wout, y_ref, *rest, decode):
    if decode:
        v_ref, s_ref = rest
    else:
        (s_ref,) = rest
    tm = x_ref.shape[0]
    h = _gelu(_bdot(x_ref[...], win[...]))
    u = h[:, :A_INNER]
    v = _ln(h[:, A_INNER:], lg[...], lb[...])
    if decode:
        v_ref[...] = v
        nb = tm // 4
        for t in range(4):
            acc = bs[t:t + 1, :]
            for s in range(t + 1):
                acc = acc + ws[4 * t + s:4 * t + s + 1, :] * v[s * nb:(s + 1) * nb, :]
            s_ref[t * nb:(t + 1) * nb, :] = acc
    else:
        for c in range(tm // A_CHUNK):
            r = slice(c * A_CHUNK, (c + 1) * A_CHUNK)
            for g in range(A_GROUPS):
                l = slice(g * 128, (g + 1) * 128)
                s_ref[r, l] = jnp.dot(ws[g], v[r, l].astype(BF16), preferred_element_type=F32) + bs[:, l]
    y_ref[...] = _bdot(u * s_ref[...], wout[...])


def _mix_a(x, win, lg, lb, ws, bs, wout, decode):
    n = x.shape[0]
    tm = n if decode else ROW_TILE
    out_shape = [jax.ShapeDtypeStruct((n, D_MODEL), F32)]
    out_specs = [_rows(tm, D_MODEL)]
    if decode:
        out_shape.append(jax.ShapeDtypeStruct((n, A_INNER), F32))
        out_specs.append(_rows(tm, A_INNER))
    res = pl.pallas_call(
        functools.partial(_mixa_body, decode=decode),
        grid=(n // tm,),
        in_specs=[_rows(tm, D_MODEL), _resident(win.shape), _resident((1, A_INNER)), _resident((1, A_INNER)),
                  _resident(ws.shape), _resident(bs.shape), _resident(wout.shape)],
        out_specs=out_specs, out_shape=out_shape,
        scratch_shapes=[pltpu.VMEM((tm, A_INNER), F32)],
        compiler_params=_cparams("parallel"), name="mix_a_dec" if decode else "mix_a")(x, win, lg, lb, ws, bs, wout)
    return res


def _gdn_post(act, x, wz, wba, arow, dtrow, q_o, k_o, v_o, z_o, ba_o):
    for h in range(GDN_H):
        l = slice(h * GDN_D, (h + 1) * GDN_D)
        qh = act[:, l]
        q_o[:, l] = qh * lax.rsqrt(jnp.sum(qh * qh, axis=-1, keepdims=True) + NORM_EPS) * (GDN_D ** -0.5)
        kh = act[:, D_MODEL + h * GDN_D:D_MODEL + (h + 1) * GDN_D]
        k_o[:, l] = kh * lax.rsqrt(jnp.sum(kh * kh, axis=-1, keepdims=True) + NORM_EPS)
    v_o[...] = act[:, 2 * D_MODEL:]
    z_o[...] = _bdot(x, wz[...])
    lg = _bdot(x, wba[...])
    lane = lax.broadcasted_iota(jnp.int32, lg.shape, 1)
    ba_o[...] = jnp.where(lane < GDN_H, _sigmoid(lg), arow[...] * _softplus(lg + dtrow[...]))


def _gdn_pre_body(x_ref, xh_ref, wqkv, wz, wba, cw, arow, dtrow, q_o, k_o, v_o, z_o, ba_o, tail_o, ext,
                  *, tiles_per_seq):
    tm = x_ref.shape[0]
    x = x_ref[...]
    pre = _bdot(x, wqkv[...])
    halo = _bdot(xh_ref[...], wqkv[...])
    first = pl.program_id(0) % tiles_per_seq == 0
    ext[0:8, :] = jnp.where(first, 0.0, halo)
    ext[8:8 + tm, :] = pre
    tail_o[0] = pre[tm - 8:, :]
    conv = ext[pl.ds(5, tm), :] * cw[0:1, :]
    for j in range(1, GDN_CONV):
        conv = conv + ext[pl.ds(5 + j, tm), :] * cw[j:j + 1, :]
    _gdn_post(_silu(conv), x, wz, wba, arow, dtrow, q_o, k_o, v_o, z_o, ba_o)


def _gdn_pre_dec_body(x_ref, buf_ref, wqkv, wz, wba, cw, arow, dtrow, q_o, k_o, v_o, z_o, ba_o, nbuf_o, ext):
    n = x_ref.shape[0]
    nb = n // 4
    x = x_ref[...]
    ext[0:3 * nb, :] = buf_ref[...]
    ext[3 * nb:3 * nb + n, :] = _bdot(x, wqkv[...])
    nbuf_o[...] = ext[4 * nb:7 * nb, :]
    conv = ext[0:n, :] * cw[0:1, :]
    for j in range(1, GDN_CONV):
        conv = conv + ext[j * nb:j * nb + n, :] * cw[j:j + 1, :]
    _gdn_post(_silu(conv), x, wz, wba, arow, dtrow, q_o, k_o, v_o, z_o, ba_o)


def _gdn_pre(x, wqkv, wz, wba, cw, arow, dtrow, seq_len=None, conv_buf=None):
    n = x.shape[0]
    w_specs = [_resident(wqkv.shape), _resident(wz.shape), _resident(wba.shape), _resident(cw.shape),
               _resident((1, 128)), _resident((1, 128))]
    outs = [jax.ShapeDtypeStruct((n, D_MODEL), F32)] * 4 + [jax.ShapeDtypeStruct((n, 128), F32)]
    if conv_buf is None:
        tm = ROW_TILE
        tps = seq_len // tm
        nseq = n // seq_len
        return pl.pallas_call(
            functools.partial(_gdn_pre_body, tiles_per_seq=tps), grid=(n // tm,),
            in_specs=[_rows(tm, D_MODEL),
                      pl.BlockSpec((8, D_MODEL), lambda i: (jnp.maximum(i * (tm // 8) - 1, 0), 0))] + w_specs,
            out_specs=[_rows(tm, D_MODEL)] * 4 + [_rows(tm, 128),
                                                   pl.BlockSpec((1, 8, GDN_QKV), lambda i: (i // tps, 0, 0))],
            out_shape=outs + [jax.ShapeDtypeStruct((nseq, 8, GDN_QKV), F32)],
            scratch_shapes=[pltpu.VMEM((tm + 8, GDN_QKV), F32)],
            compiler_params=_cparams("arbitrary"), name="gdn_pre")(x, x, wqkv, wz, wba, cw, arow, dtrow)
    nb = n // 4
    return pl.pallas_call(
        _gdn_pre_dec_body, grid=(1,),
        in_specs=[_rows(n, D_MODEL), _rows(3 * nb, GDN_QKV)] + w_specs,
        out_specs=[_rows(n, D_MODEL)] * 4 + [_rows(n, 128), _rows(3 * nb, GDN_QKV)],
        out_shape=outs + [jax.ShapeDtypeStruct((3 * nb, GDN_QKV), F32)],
        scratch_shapes=[pltpu.VMEM((7 * nb, GDN_QKV), F32)],
        compiler_params=_cparams("arbitrary"), name="gdn_pre_dec")(x, conv_buf, wqkv, wz, wba, cw, arow, dtrow)


def _gdn_scan_body(q_ref, k_ref, v_ref, ba_ref, s0_ref, o_ref, s_ref):
    c = q_ref.shape[0]
    heads = range(GDN_H)

    @pl.when(pl.program_id(1) == 0)
    def _():
        s_ref[...] = s0_ref[...]

    ba = ba_ref[...]
    ri = lax.broadcasted_iota(jnp.int32, (c, c), 0)
    ci = lax.broadcasted_iota(jnp.int32, (c, c), 1)
    incl = ci <= ri
    strict = ci < ri
    eye = (ri == ci).astype(F32)
    gcum_all = _hdot(incl.astype(F32), ba)
    lanes = [slice(h * GDN_D, (h + 1) * GDN_D) for h in heads]
    gc, decay, kbeta, inv, pw = [], [], [], [], []
    for h in heads:
        kh = k_ref[:, lanes[h]]
        g = gcum_all[:, GDN_H + h:GDN_H + h + 1]
        g_row = jnp.sum(g * eye, axis=0, keepdims=True)
        d = jnp.where(incl, jnp.exp(jnp.where(incl, g - g_row, 0.0)), 0.0)
        kb = kh * ba[:, h:h + 1]
        a = jnp.where(strict, _bdot_nt(kb, kh) * d, 0.0)
        gc.append(g), decay.append(d), kbeta.append(kb), inv.append(eye - a), pw.append(a)
    pw = [_split(p) for p in pw]
    for _ in range(int(math.log2(c)) - 1):
        pw = [_split(_dot3(p, p)) for p in pw]
        inv = [t + _dot3(_split(t), p) for t, p in zip(inv, pw)]
    u = [_bdot(inv[h], v_ref[:, lanes[h]] * ba[:, h:h + 1]) for h in heads]
    w = [_bdot(inv[h], kbeta[h] * jnp.exp(gc[h])) for h in heads]
    qk = [jnp.where(incl, _bdot_nt(q_ref[:, lanes[h]], k_ref[:, lanes[h]]) * decay[h], 0.0) for h in heads]
    s = [s_ref[0, h] for h in heads]
    v_new = [u[h] - _bdot(w[h], s[h]) for h in heads]
    for h in heads:
        o_ref[:, lanes[h]] = _bdot(q_ref[:, lanes[h]] * jnp.exp(gc[h]), s[h]) + _bdot(qk[h], v_new[h])
    for h in heads:
        g_last = gc[h][c - 1:c, :]
        kd = k_ref[:, lanes[h]] * jnp.exp(g_last - gc[h])
        s_ref[0, h] = s[h] * jnp.exp(g_last) + _bdot_tn(kd, v_new[h])


def _gdn_scan(q, k, v, ba, s0, chunk):
    n = q.shape[0]
    nseq = s0.shape[0]
    nch = n // nseq // chunk
    row = lambda w: pl.BlockSpec((chunk, w), lambda b, c: (b * nch + c, 0))
    st = pl.BlockSpec((1, GDN_H, GDN_D, GDN_D), lambda b, c: (b, 0, 0, 0))
    return pl.pallas_call(
        _gdn_scan_body, grid=(nseq, nch),
        in_specs=[row(D_MODEL), row(D_MODEL), row(D_MODEL), row(128), st],
        out_specs=[row(D_MODEL), st],
        out_shape=[jax.ShapeDtypeStruct((n, D_MODEL), F32), jax.ShapeDtypeStruct(s0.shape, F32)],
        compiler_params=_cparams("parallel", "arbitrary"), name="gdn_scan")(q, k, v, ba, s0)


def _gdn_out_body(o_ref, z_ref, ng, wout, y_ref, t_ref):
    for h in range(GDN_H):
        l = slice(h * GDN_D, (h + 1) * GDN_D)
        oh = o_ref[:, l]
        t_ref[:, l] = oh * lax.rsqrt(jnp.mean(oh * oh, axis=-1, keepdims=True) + NORM_EPS)
    y_ref[...] = _bdot(t_ref[...] * ng[...] * _silu(z_ref[...]), wout[...])


def _gdn_out(o, z, ng, wout):
    n = o.shape[0]
    tm = ROW_TILE
    return pl.pallas_call(
        _gdn_out_body, grid=(n // tm,),
        in_specs=[_rows(tm, D_MODEL), _rows(tm, D_MODEL), _resident((1, D_MODEL)), _resident(wout.shape)],
        out_specs=_rows(tm, D_MODEL), out_shape=jax.ShapeDtypeStruct((n, D_MODEL), F32),
        scratch_shapes=[pltpu.VMEM((tm, D_MODEL), F32)],
        compiler_params=_cparams("parallel"), name="gdn_out")(o, z, ng, wout)


NSA_SCALE = NSA_DH ** -0.5
PAGE = 128


def _projt_body(x_ref, wt_ref, *o_refs):
    yt = _bdot_nt(wt_ref[...], x_ref[...])
    off = 0
    for o in o_refs:
        wdt = o.shape[1]
        o[0] = yt[off:off + wdt, :]
        off += wdt


def _projt(x, wt, widths, seq_len):
    n = x.shape[0]
    tm = ROW_TILE
    tps = seq_len // tm
    return pl.pallas_call(
        _projt_body, grid=(n // tm,),
        in_specs=[_rows(tm, D_MODEL), _resident(wt.shape)],
        out_specs=[pl.BlockSpec((1, wd, tm), lambda i: (i // tps, 0, i % tps)) for wd in widths],
        out_shape=[jax.ShapeDtypeStruct((n // seq_len, wd, seq_len), F32) for wd in widths],
        compiler_params=_cparams("parallel"), name="nsa_proj")(x, wt)


def _cmp_body(*refs, n_pages, group, paged):
    if paged:
        refs = refs[1:]
    srcs, (pet, w1, w2, o_ref, tall, flat) = refs[:-6], refs[-6:]
    n_blk = n_pages * (PAGE // NSA_L_CMP)
    slot = pl.program_id(0) % group
    row0 = pl.multiple_of(slot * n_blk, 8)
    for c in range(2):
        for gp in range(2):
            for p in range(n_pages):
                if paged:
                    slab = srcs[p][0, 0, c, 2 * gp:2 * gp + 2].reshape(2 * NSA_DH, PAGE)
                else:
                    slab = srcs[0][0, (2 * c + gp) * 128:(2 * c + gp + 1) * 128, p * PAGE:(p + 1) * PAGE]
                tall[p * PAGE:(p + 1) * PAGE, :] = (slab + pet[c]).T
            for l in range(NSA_L_CMP):
                flat[2 * c + gp, pl.ds(row0, n_blk), l * 128:(l + 1) * 128] = tall[pl.ds(l, n_blk, stride=NSA_L_CMP), :]

    @pl.when(slot == group - 1)
    def _():
        for c in range(2):
            for gp in range(2):
                hid = _gelu(_bdot(flat[2 * c + gp], w1[c]))
                o_ref[c, :, gp * 128:(gp + 1) * 128] = _bdot(hid, w2[c])


def _compress(src, pet, w1bd, w2bd, page_table=None):
    paged = page_table is not None
    if paged:
        nseq, n_pages = page_table.shape
        group = 4
        src_specs = [pl.BlockSpec((1, 1, 2, NSA_G, NSA_DH, PAGE),
                                  functools.partial(lambda b, pt, p: (0, pt[b, p], 0, 0, 0, 0), p=p))
                     for p in range(n_pages)]
        srcs = [src] * n_pages
        wmap = lambda nd: (lambda b, pt: (0,) * nd)
        omap = lambda b, pt: (0, b // group, 0)
    else:
        nseq, _, t = src.shape
        n_pages = t // PAGE
        group = 1
        src_specs = [pl.BlockSpec((1, 2 * NSA_G * NSA_DH, t), lambda b: (b, 0, 0))]
        srcs = [src]
        wmap = lambda nd: (lambda b: (0,) * nd)
        omap = lambda b: (0, b, 0)
    n_blk = n_pages * (PAGE // NSA_L_CMP)
    in_specs = src_specs + [pl.BlockSpec(a.shape, wmap(a.ndim), pipeline_mode=pl.Buffered(1)) for a in (pet, w1bd, w2bd)]
    out_spec = pl.BlockSpec((2, group * n_blk, NSA_G * NSA_DH), omap)
    scratch = [pltpu.VMEM((n_pages * PAGE, 128), F32), pltpu.VMEM((4, group * n_blk, NSA_L_CMP * 128), F32)]
    out_shape = jax.ShapeDtypeStruct((2, nseq * n_blk, NSA_G * NSA_DH), F32)
    body = functools.partial(_cmp_body, n_pages=n_pages, group=group, paged=paged)
    if paged:
        gs = pltpu.PrefetchScalarGridSpec(num_scalar_prefetch=1, grid=(nseq,), in_specs=in_specs, out_specs=out_spec,
                                          scratch_shapes=scratch)
        return pl.pallas_call(body, grid_spec=gs, out_shape=out_shape, compiler_params=_cparams("arbitrary"),
                              name="nsa_compress_paged")(page_table, *srcs, pet, w1bd, w2bd)
    return pl.pallas_call(body, grid=(nseq,), in_specs=in_specs, out_specs=out_spec, out_shape=out_shape,
                          scratch_shapes=scratch, compiler_params=_cparams("arbitrary"),
                          name="nsa_compress")(*srcs, pet, w1bd, w2bd)


def _select(ps, jq, n_slc):
    n_p, w = ps.shape
    j = lax.broadcasted_iota(jnp.int32, (n_p, w), 0)
    forced = (j == 0) | (j == jq) | (j == jq - 1)
    sc = jnp.where(forced, NSA_FORCE, jnp.where(j > jq, -1.0, ps))
    sc = jnp.where(j >= n_slc, -2.0, sc)
    rank = jnp.zeros((n_p, w), F32)
    for i in range(n_slc):
        r = sc[i:i + 1, :]
        rank = rank + jnp.where((r > sc) | ((r == sc) & (j > i)), 1.0, 0.0)
    return jnp.where(rank < NSA_N_SEL, 1.0, 0.0)


def _softmax_keys(s, mask):
    m = jnp.max(s, axis=0, keepdims=True)
    e = jnp.exp(s - m)
    return jnp.where(mask, e / jnp.sum(e, axis=0, keepdims=True), 0.0)


def _q4(q_ref, g0, tq):
    return jnp.concatenate([q_ref[0, (g0 + hh) * NSA_DH:(g0 + hh + 1) * NSA_DH, :] for hh in range(NSA_HG)], axis=1)


def _cmp_attn_body(q_ref, kc_ref, vc_ref, b_ref, pair_ref, o_ref, sel_ref, *, n_slc):
    tq = q_ref.shape[-1]
    lanes = NSA_HG * tq
    kc, vc = kc_ref[0], vc_ref[0]
    qpos = pl.program_id(1) * tq + lax.broadcasted_iota(jnp.int32, (1, tq), 1)
    for g in range(NSA_G):
        f = slice(g * NSA_DH, (g + 1) * NSA_DH)
        bias = b_ref[0, :, g * lanes:(g + 1) * lanes]
        mask = bias > 0.5 * NEG_INF
        s = jnp.where(mask, _bdot(kc[:, f], _q4(q_ref, g * NSA_HG, tq)) * NSA_SCALE + bias, NEG_INF)
        p = _softmax_keys(s, mask)
        o = _bdot_tn(vc[:, f], p)
        ps = p[:, :tq]
        for hh in range(NSA_HG):
            o_ref[0, (g * NSA_HG + hh) * NSA_DH:(g * NSA_HG + hh + 1) * NSA_DH, :] = o[:, hh * tq:(hh + 1) * tq]
            if hh:
                ps = ps + p[:, hh * tq:(hh + 1) * tq]
        sel_ref[0, g] = _select(_hdot(pair_ref[...], ps), qpos // NSA_L_SLC, n_slc)


def _cmp_attn(qt, kc, vc, bias, pair, n_slc):
    b, _, t = qt.shape
    tq = ATT_TILE
    n_cmp = kc.shape[1]
    n_p = pair.shape[0]
    qs = pl.BlockSpec((1, D_MODEL, tq), lambda b_, i: (b_, 0, i))
    ks = pl.BlockSpec((1, n_cmp, NSA_G * NSA_DH), lambda b_, i: (b_, 0, 0))
    return pl.pallas_call(
        functools.partial(_cmp_attn_body, n_slc=n_slc), grid=(b, t // tq),
        in_specs=[qs, ks, ks, pl.BlockSpec((1, n_cmp, NSA_H * tq), lambda b_, i: (i, 0, 0)),
                  pl.BlockSpec(pair.shape, lambda b_, i: (0, 0))],
        out_specs=[qs, pl.BlockSpec((1, NSA_G, n_p, tq), lambda b_, i: (b_, 0, 0, i))],
        out_shape=[jax.ShapeDtypeStruct(qt.shape, F32), jax.ShapeDtypeStruct((b, NSA_G, n_p, t), F32)],
        compiler_params=_cparams("parallel", "parallel"), name="nsa_cmp_attn")(qt, kc, vc, bias, pair)


def _sw_attn_body(q_ref, ks_ref, vs_ref, sel_ref, b_ref, kw_ref, vw_ref, os_ref, ow_ref, negs, m_s, acc_s):
    tq = SW_TILE
    lanes = NSA_HG * tq
    n_tab = b_ref.shape[0]
    qt = pl.program_id(2)
    q0 = qt * tq
    q = (_q4(q_ref, 0, tq) * NSA_SCALE).astype(BF16)
    negs[...] = (jnp.concatenate([sel_ref[0, 0]] * NSA_HG, axis=1) - 1.0) * (-NEG_INF)
    ones = jnp.ones((8, n_tab), BF16)

    def sel_neg(k0, size):
        rows = [jnp.broadcast_to(negs[pl.ds(k0 // NSA_L_SLC + i, 1), :], (NSA_L_SLC, lanes))
                for i in range(size // NSA_L_SLC)]
        return jnp.concatenate(rows, axis=0)

    def scores(k_ref, k0, size, add):
        return _bdot_tn(k_ref[0, :, pl.ds(k0, size)], q) + add

    def update(s, v_ref, k0, size):
        vt = jnp.concatenate([v_ref[0, :, pl.ds(k0, size)].astype(BF16), ones[:, :size]], axis=0)
        m_old = m_s[...]
        m_new = jnp.maximum(m_old, jnp.max(s, axis=0, keepdims=True))
        p = jnp.exp(s - m_new).astype(BF16)
        acc_s[...] = jnp.exp(m_old - m_new) * acc_s[...] + jnp.dot(vt, p, preferred_element_type=F32)
        m_s[...] = m_new

    def near(k_ref, v_ref, size, masked):
        k0 = pl.multiple_of(q0 + tq - size, tq)
        add = b_ref[n_tab - size:, :]
        if masked:
            add = add + sel_neg(k0, size)
        update(scores(k_ref, k0, size, add), v_ref, k0, size)

    def init():
        m_s[...] = jnp.full(m_s.shape, NEG_INF, F32)
        acc_s[...] = jnp.zeros(acc_s.shape, F32)

    def write(o_ref):
        acc = acc_s[...]
        l = acc[NSA_DH:NSA_DH + 1, :]
        o = acc[:NSA_DH, :] / jnp.where(l > 0.0, l, 1.0)
        for hh in range(NSA_HG):
            o_ref[0, hh * NSA_DH:(hh + 1) * NSA_DH, :] = o[:, hh * tq:(hh + 1) * tq]

    init()
    pl.when(qt == 0)(lambda: near(ks_ref, vs_ref, tq, True))
    pl.when(qt >= 1)(lambda: near(ks_ref, vs_ref, 2 * tq, True))
    n_far = jnp.maximum(qt - 1, 0) * tq
    n_big = n_far // FAR_TILE

    def far_pair(i, carry):
        ka = pl.multiple_of(2 * i * FAR_TILE, FAR_TILE)
        kb = pl.multiple_of((2 * i + 1) * FAR_TILE, FAR_TILE)
        sa = scores(ks_ref, ka, FAR_TILE, sel_neg(ka, FAR_TILE))
        sb = scores(ks_ref, kb, FAR_TILE, sel_neg(kb, FAR_TILE))
        update(sa, vs_ref, ka, FAR_TILE)
        update(sb, vs_ref, kb, FAR_TILE)
        return carry

    lax.fori_loop(0, n_big // 2, far_pair, 0)

    def far_one(k0, size):
        update(scores(ks_ref, k0, size, sel_neg(k0, size)), vs_ref, k0, size)

    pl.when(n_big % 2 == 1)(lambda: far_one(pl.multiple_of((n_big - 1) * FAR_TILE, FAR_TILE), FAR_TILE))
    pl.when(n_far % FAR_TILE != 0)(lambda: far_one(pl.multiple_of(n_big * FAR_TILE, tq), tq))
    write(os_ref)

    init()
    for j in range(1, n_tab // tq):
        pl.when(qt == j - 1)(functools.partial(near, kw_ref, vw_ref, j * tq, False))
    pl.when(qt >= n_tab // tq - 1)(lambda: near(kw_ref, vw_ref, n_tab, False))
    write(ow_ref)


def _sw_attn(qt, kvt, wint, sel, btab):
    b, _, t = qt.shape
    tq = SW_TILE
    lanes = NSA_HG * tq
    qs = pl.BlockSpec((1, NSA_HG * NSA_DH, tq), lambda b_, g_, i: (b_, g_, i))
    feat = lambda blk0: pl.BlockSpec((1, NSA_DH, t), lambda b_, g_, i: (b_, blk0 + g_, 0))
    return pl.pallas_call(
        _sw_attn_body, grid=(b, NSA_G, t // tq),
        in_specs=[qs, feat(2 * NSA_G), feat(3 * NSA_G),
                  pl.BlockSpec((1, 1, sel.shape[2], tq), lambda b_, g_, i: (b_, g_, 0, i)),
                  pl.BlockSpec((btab.shape[0], lanes), lambda b_, g_, i: (0, g_)),
                  feat(0), feat(NSA_G)],
        out_specs=[qs, qs],
        out_shape=[jax.ShapeDtypeStruct(qt.shape, F32)] * 2,
        scratch_shapes=[pltpu.VMEM((sel.shape[2], lanes), F32), pltpu.VMEM((1, lanes), F32),
                        pltpu.VMEM((NSA_DH + 8, lanes), F32)],
        compiler_params=_cparams("parallel", "parallel", "arbitrary"),
        name="nsa_sw_attn")(qt, kvt, kvt, sel, btab, wint, wint)


def _attend(q, kts, vts, adds):
    s = [_bdot_tn(kt, q) + add for kt, add in zip(kts, adds)]
    m = functools.reduce(jnp.maximum, [jnp.max(x, axis=0, keepdims=True) for x in s])
    p = [jnp.exp(x - m) for x in s]
    l = functools.reduce(jnp.add, [jnp.sum(x, axis=0, keepdims=True) for x in p])
    acc = functools.reduce(jnp.add, [_bdot(vt, x) for vt, x in zip(vts, p)])
    return acc / l


def _dec_attn_body(pt_ref, *refs, n_pages, n_slc, q_pos0, tq):
    pages = refs[:n_pages]
    (q_ref, kc_ref, vc_ref, bc_ref, pair_ref, hsum_ref, knew_ref, vnew_ref, et_ref, bs_ref, win_ref, kwnew_ref,
     vwnew_ref, bw_ref, oc_ref, os_ref, ow_ref) = refs[n_pages:]
    feats, lanes = q_ref.shape[1:]
    q = (q_ref[0] * NSA_SCALE).astype(BF16)

    bias = bc_ref[...]
    mask = bias > 0.5 * NEG_INF
    p = _softmax_keys(jnp.where(mask, _bdot(kc_ref[0], q) + bias, NEG_INF), mask)
    oc_ref[0] = _bdot_tn(vc_ref[0], p)
    ps = _hdot(pair_ref[...], _hdot(p, hsum_ref[...]))
    lane = lax.broadcasted_iota(jnp.int32, (1, lanes), 1)
    sel = _select(ps, (q_pos0 + lane % tq) // NSA_L_SLC, n_slc).astype(BF16)

    kts = [pages[j][0, 0, 0].reshape(feats, PAGE) for j in range(n_pages)] + [knew_ref[0]]
    vts = [pages[j][0, 0, 1].reshape(feats, PAGE) for j in range(n_pages)] + [vnew_ref[0]]
    adds = [bs_ref[j] + (jnp.dot(et_ref[j], sel, preferred_element_type=F32) - 1.0) * (-NEG_INF)
            for j in range(n_pages + 1)]
    os_ref[0] = _attend(q, kts, vts, adds)

    n_wt = win_ref.shape[-1] // PAGE
    kts = [win_ref[0, 0, :, j * PAGE:(j + 1) * PAGE] for j in range(n_wt)] + [kwnew_ref[0]]
    vts = [win_ref[0, 1, :, j * PAGE:(j + 1) * PAGE] for j in range(n_wt)] + [vwnew_ref[0]]
    ow_ref[0] = _attend(q, kts, vts, [bw_ref[j] for j in range(n_wt + 1)])


def _dec_attn(page_table, cache_t, qbd, kc, vc, bias_c, pair, hsum, knew, vnew, et, bsel, win_t, kwnew, vwnew, bwin,
              n_slc, q_pos0, tq):
    nseq, n_pages = page_table.shape
    per_seq = lambda a: pl.BlockSpec((1,) + a.shape[1:], lambda b, pt: (b,) + (0,) * (a.ndim - 1))
    whole = lambda a: pl.BlockSpec(a.shape, lambda b, pt: (0,) * a.ndim, pipeline_mode=pl.Buffered(1))
    page_specs = [pl.BlockSpec((1, 1, 2, NSA_G, NSA_DH, PAGE),
                               functools.partial(lambda b, pt, p: (0, pt[b, p], 1, 0, 0, 0), p=p))
                  for p in range(n_pages)]
    gs = pltpu.PrefetchScalarGridSpec(
        num_scalar_prefetch=1, grid=(nseq,),
        in_specs=page_specs + [per_seq(qbd), per_seq(kc), per_seq(vc), whole(bias_c), whole(pair), whole(hsum),
                               per_seq(knew), per_seq(vnew), whole(et), whole(bsel), per_seq(win_t), per_seq(kwnew),
                               per_seq(vwnew), whole(bwin)],
        out_specs=[per_seq(qbd)] * 3)
    return pl.pallas_call(
        functools.partial(_dec_attn_body, n_pages=n_pages, n_slc=n_slc, q_pos0=q_pos0, tq=tq), grid_spec=gs,
        out_shape=[jax.ShapeDtypeStruct(qbd.shape, F32)] * 3, compiler_params=_cparams("parallel"),
        name="nsa_dec_attn")(page_table, *([cache_t] * n_pages), qbd, kc, vc, bias_c, pair, hsum, knew, vnew, et, bsel,
                             win_t, kwnew, vwnew, bwin)


def _nsa_out_body(x_ref, oc_ref, os_ref, ow_ref, wg, gb, wout, y_ref):
    gates = _sigmoid(_bdot_nt(wg[...], x_ref[...]) + gb[...])
    parts = []
    for h in range(NSA_H):
        r = slice(h * NSA_DH, (h + 1) * NSA_DH)
        parts.append(gates[h:h + 1, :] * oc_ref[0, r, :] + gates[NSA_H + h:NSA_H + h + 1, :] * os_ref[0, r, :]
                     + gates[2 * NSA_H + h:2 * NSA_H + h + 1, :] * ow_ref[0, r, :])
    y_ref[...] = _bdot_tn(jnp.concatenate(parts, axis=0), wout[...])


def _nsa_out(x, oc, osl, ow, wg, gb, wout):
    n = x.shape[0]
    tm = ROW_TILE
    tps = oc.shape[2] // tm
    col = pl.BlockSpec((1, D_MODEL, tm), lambda i: (i // tps, 0, i % tps))
    return pl.pallas_call(
        _nsa_out_body, grid=(n // tm,),
        in_specs=[_rows(tm, D_MODEL), col, col, col, _resident(wg.shape), _resident(gb.shape), _resident(wout.shape)],
        out_specs=_rows(tm, D_MODEL), out_shape=jax.ShapeDtypeStruct((n, D_MODEL), F32),
        compiler_params=_cparams("parallel"), name="nsa_out")(x, oc, osl, ow, wg, gb, wout)


def _t5_bucket(dist):
    n = jnp.maximum(dist, 0)
    max_exact = NUM_BUCKETS // 2
    nf = jnp.maximum(n, 1).astype(F32)
    large = max_exact + (jnp.log(nf / max_exact) / math.log(MAX_DISTANCE / max_exact)
                         * (NUM_BUCKETS - max_exact)).astype(jnp.int32)
    return jnp.where(n < max_exact, n, jnp.minimum(large, NUM_BUCKETS - 1))


def _bias_table(tab, dist, valid):
    onehot = (_t5_bucket(dist)[..., None] == jnp.arange(NUM_BUCKETS)).astype(F32)
    bias = jnp.einsum("...k,kh->...h", onehot, tab.astype(F32), precision=HIGHEST)
    bias = jnp.swapaxes(jnp.where(valid[..., None], bias, NEG_INF), -1, -2)
    return bias.reshape(bias.shape[:-2] + (NSA_H * dist.shape[-1],))


def _nsa_weights(w_in, gate_b, pe, w1, w2, w_out):
    n_att = NSA_H * NSA_DH + 6 * NSA_G * NSA_DH
    eye2 = jnp.eye(2, dtype=F32)
    w1r = w1.reshape(2, NSA_L_CMP, NSA_DH, -1)
    w1bd = jnp.einsum("cldj,pq->clpdqj", w1r, eye2).reshape(2, NSA_L_CMP * 2 * NSA_DH, 2 * w1.shape[-1])
    w2bd = jnp.einsum("cjd,pq->cpjqd", w2, eye2).reshape(2, 2 * w2.shape[1], 2 * NSA_DH)
    pet = jnp.broadcast_to(pe.transpose(0, 2, 1)[:, None, :, None, :],
                           (2, 2, NSA_DH, PAGE // NSA_L_CMP, NSA_L_CMP)).reshape(2, 2 * NSA_DH, PAGE)
    return dict(wt_in=w_in[:, :n_att].T.astype(BF16), w_gate=w_in[:, n_att:].T.astype(BF16),
                gate_b=gate_b.reshape(-1, 1), pet=pet, w1bd=w1bd.astype(BF16), w2bd=w2bd.astype(BF16),
                w_out=w_out.astype(BF16))


_NSA_WIDTHS = (NSA_H * NSA_DH, 4 * NSA_G * NSA_DH, 2 * NSA_G * NSA_DH)


def _pair_matrix(n_cmp, n_slc_p):
    return (jnp.arange(n_cmp)[None, :] // (NSA_L_SLC // NSA_L_CMP) == jnp.arange(n_slc_p)[:, None]).astype(F32)


def _block_expand(n_tiles, n_slc_p):
    et = jnp.arange(n_tiles * ATT_TILE)[:, None] // NSA_L_SLC == jnp.arange(n_slc_p)[None, :]
    return et.astype(BF16).reshape(n_tiles, ATT_TILE, n_slc_p)


def _nsa_prompt(x, seq_len, w, tab):
    nseq = x.shape[0] // seq_len
    tq = ATT_TILE
    nq = seq_len // tq
    n_cmp = seq_len // NSA_L_CMP
    n_slc = -(-seq_len // NSA_L_SLC)
    n_slc_p = -(-n_slc // 8) * 8
    qt, kvt, wint = _projt(x, w["wt_in"], _NSA_WIDTHS, seq_len)
    cmp = _compress(kvt, w["pet"], w["w1bd"], w["w2bd"]).reshape(2, nseq, n_cmp, NSA_G * NSA_DH)

    qpos = (jnp.arange(nq, dtype=jnp.int32) * tq)[:, None, None] + jnp.arange(tq, dtype=jnp.int32)[None, None, :]
    dist = qpos - (jnp.arange(n_cmp, dtype=jnp.int32) * NSA_L_CMP + NSA_L_CMP - 1)[None, :, None]
    oc, sel = _cmp_attn(qt, cmp[0], cmp[1], _bias_table(tab, dist, dist >= 0), _pair_matrix(n_cmp, n_slc_p), n_slc)

    n_tab = NSA_WINDOW + SW_TILE
    d_t = (n_tab - SW_TILE + jnp.arange(SW_TILE, dtype=jnp.int32))[None, :] - jnp.arange(n_tab, dtype=jnp.int32)[:, None]
    btab = _bias_table(tab, d_t, (d_t >= 0) & (d_t < NSA_WINDOW))
    bfar = _bias_table(tab, jnp.full((1, SW_TILE), 4 * MAX_DISTANCE, jnp.int32), jnp.ones((1, SW_TILE), bool))
    osl, ow = _sw_attn(qt, kvt, wint, sel, jnp.where(btab > 0.5 * NEG_INF, btab - bfar, NEG_INF))
    y = _nsa_out(x, oc, osl, ow, w["w_gate"], w["gate_b"], w["w_out"])
    kv_new = kvt.reshape(nseq, 4, NSA_G, NSA_DH, seq_len).transpose(0, 4, 1, 2, 3)
    win_new = wint.reshape(nseq, 2, NSA_G, NSA_DH, seq_len).transpose(0, 4, 1, 2, 3)
    return y, kv_new, win_new


def _nsa_decode(x, n_tok, q_pos0, cache, win_past, page_table, w, tab):
    n = x.shape[0]
    nseq = n // n_tok
    tq = 8
    lanes = NSA_H * tq
    feats = NSA_G * NSA_DH
    n_pages = page_table.shape[1]
    past = n_pages * PAGE
    n_cmp = (past + n_tok) // NSA_L_CMP
    n_slc = -(-(past + n_tok) // NSA_L_SLC)
    n_slc_p = -(-n_slc // 8) * 8
    cache_t = cache.transpose(0, 2, 3, 4, 1)[None]
    win_t = win_past.transpose(0, 2, 3, 4, 1).reshape(nseq, 2, feats, win_past.shape[1])
    qt, kvt, wint = _projt(x, w["wt_in"], _NSA_WIDTHS, n)

    cmp = _compress(cache_t, w["pet"], w["w1bd"], w["w2bd"], page_table).reshape(2, nseq, n_cmp, feats)

    qg = qt[0].reshape(NSA_G, NSA_HG, NSA_DH, n_tok, nseq).transpose(4, 0, 2, 1, 3)
    qg = jnp.pad(qg, ((0, 0),) * 4 + ((0, tq - n_tok),)).reshape(nseq, NSA_G, NSA_DH, NSA_HG * tq)
    qbd = jnp.einsum("bgdl,gh->bgdhl", qg, jnp.eye(NSA_G, dtype=F32)).reshape(nseq, feats, lanes)
    new_t = lambda a: jnp.pad(a.reshape(feats, n_tok, nseq).transpose(2, 0, 1), ((0, 0), (0, 0), (0, PAGE - n_tok)))
    kv_new = kvt[0].reshape(4, NSA_G, NSA_DH, n_tok, nseq)
    win_new = wint[0].reshape(2, NSA_G, NSA_DH, n_tok, nseq)

    lane = jnp.arange(lanes, dtype=jnp.int32)
    qpos = q_pos0 + jnp.arange(tq, dtype=jnp.int32)[None, None, :]

    def bias_of(kpos, extra):
        d = qpos - kpos[:, :, None]
        return _bias_table(tab, d, (d >= 0) & extra(d, kpos[:, :, None]))

    tile_pos = lambda nt: (jnp.arange(nt, dtype=jnp.int32) * PAGE)[:, None] + jnp.arange(PAGE, dtype=jnp.int32)[None, :]
    kend = (jnp.arange(n_cmp, dtype=jnp.int32) * NSA_L_CMP + NSA_L_CMP - 1)[None, :]
    bias_c = bias_of(kend, lambda d, k: True)[0]
    bsel = bias_of(tile_pos(n_pages + 1), lambda d, k: k < past + n_tok)
    n_wt = win_past.shape[1] // PAGE
    wpos = jnp.concatenate([q_pos0 - n_wt * PAGE + tile_pos(n_wt), q_pos0 + tile_pos(1)])
    bwin = bias_of(wpos, lambda d, k: (d < NSA_WINDOW) & (k >= 0) & (k < q_pos0 + n_tok))
    hsum = ((lane[:, None] // (NSA_HG * tq) == lane[None, :] // (NSA_HG * tq))
            & (lane[:, None] % tq == lane[None, :] % tq)).astype(F32)

    outs = _dec_attn(page_table, cache_t, qbd, cmp[0], cmp[1], bias_c, _pair_matrix(n_cmp, n_slc_p), hsum,
                     new_t(kv_new[2]), new_t(kv_new[3]), _block_expand(n_pages + 1, n_slc_p), bsel, win_t,
                     new_t(win_new[0]), new_t(win_new[1]), bwin, n_slc, q_pos0, tq)

    def own_group(o):
        o = o.reshape(nseq, NSA_G, NSA_DH, NSA_G, NSA_HG, tq)
        o = jnp.diagonal(o, axis1=1, axis2=3)[:, :, :, :n_tok]
        return o.transpose(4, 2, 1, 3, 0).reshape(1, D_MODEL, n)

    y = _nsa_out(x, *(own_group(o) for o in outs), w["w_gate"], w["gate_b"], w["w_out"])
    return y, kv_new.transpose(4, 3, 0, 1, 2), win_new.transpose(4, 3, 0, 1, 2)


def kernel(x_prompt, x_sample, state_gdn_s, state_gdn_conv, cache_nsa_kv, cache_nsa_win, page_table, p_prompt,
           p_sample, ln_g, ln_b, ffn_w_up, ffn_w_down, ple_w_gate, ple_w_proj, a_w_in, a_ln_g, a_ln_b, a_w_s, a_b_s,
           a_w_out, gdn_w_in, gdn_conv_w, gdn_a_log, gdn_dt_bias, gdn_norm_g, gdn_w_out, nsa_w_in, nsa_gate_b,
           nsa_cmp_pe, nsa_cmp_w1, nsa_cmp_w2, nsa_w_out, t5_bias_table):
    bp, tp, _ = x_prompt.shape
    bs, ts, _ = x_sample.shape
    bf = lambda a: a.astype(BF16)
    row = lambda a: a.reshape(1, -1)
    xp = x_prompt.reshape(bp * tp, D_MODEL)
    xs = x_sample.transpose(1, 0, 2).reshape(ts * bs, D_MODEL)
    pp = p_prompt.reshape(DEPTH, bp * tp, -1)
    ps = p_sample.transpose(0, 2, 1, 3).reshape(DEPTH, ts * bs, -1)
    outs = {k: [] for k in ("a_v", "s_p", "c_p", "s_s", "c_s", "kv_p", "win_p", "kv_s", "win_s")}
    ia = ib = ic = 0
    for i in range(DEPTH):
        wu, wd = bf(ffn_w_up[i, 0]), bf(ffn_w_down[i, 0])
        xp = _ffn(xp, wu, wd, row(ln_g[i, 0]), row(ln_b[i, 0]))
        xs = _ffn(xs, wu, wd, row(ln_g[i, 0]), row(ln_b[i, 0]))
        kind = i % 3
        if kind == 0:
            causal = jnp.tril(jnp.ones((A_CHUNK, A_CHUNK), bool))
            ws = bf(jnp.where(causal, a_w_s[ia], 0.0))
            bexp = jnp.repeat(a_b_s[ia].T, 128, axis=1)
            win, wout = bf(a_w_in[ia]), bf(a_w_out[ia])
            lg, lb = row(a_ln_g[ia]), row(a_ln_b[ia])
            (yp,) = _mix_a(xp, win, lg, lb, ws, bexp, wout, decode=False)
            ws_dec = jnp.repeat(a_w_s[ia][:, :ts, :ts].transpose(1, 2, 0).reshape(ts * ts, A_GROUPS), 128, axis=1)
            ys, v_rows = _mix_a(xs, win, lg, lb, ws_dec, bexp[:8], wout, decode=True)
            outs["a_v"].append(v_rows.reshape(ts, bs, A_INNER).transpose(1, 0, 2))
            ia += 1
        elif kind == 1:
            w = gdn_w_in[ib]
            wqkv, wz = bf(w[:, :GDN_QKV]), bf(w[:, GDN_QKV:GDN_QKV + D_MODEL])
            wba = bf(jnp.pad(w[:, GDN_QKV + D_MODEL:], ((0, 0), (0, 128 - 2 * GDN_H))))
            arow = jnp.pad(-jnp.exp(gdn_a_log[ib]), (GDN_H, 128 - 2 * GDN_H)).reshape(1, 128)
            dtrow = jnp.pad(gdn_dt_bias[ib], (GDN_H, 128 - 2 * GDN_H)).reshape(1, 128)
            ng, wout = row(jnp.tile(gdn_norm_g[ib], GDN_H)), bf(gdn_w_out[ib])
            q, k, v, z, ba, tail = _gdn_pre(xp, wqkv, wz, wba, gdn_conv_w[ib], arow, dtrow, seq_len=tp)
            o, s_p = _gdn_scan(q, k, v, ba, jnp.zeros((bp, GDN_H, GDN_D, GDN_D), F32), GDN_CHUNK)
            yp = _gdn_out(o, z, ng, wout)
            outs["s_p"].append(s_p)
            outs["c_p"].append(tail[:, 8 - (GDN_CONV - 1):])
            buf_t = state_gdn_conv[ib].transpose(1, 0, 2).reshape((GDN_CONV - 1) * bs, GDN_QKV)
            q, k, v, z, ba, nbuf = _gdn_pre(xs, wqkv, wz, wba, gdn_conv_w[ib], arow, dtrow, conv_buf=buf_t)
            seq8 = lambda a: jnp.pad(a.reshape(ts, bs, -1).transpose(1, 0, 2), ((0, 0), (0, 8 - ts), (0, 0))).reshape(
                bs * 8, -1)
            o, s_s = _gdn_scan(seq8(q), seq8(k), seq8(v), seq8(ba), state_gdn_s[ib], 8)
            o = o.reshape(bs, 8, D_MODEL)[:, :ts].transpose(1, 0, 2).reshape(ts * bs, D_MODEL)
            ys = _gdn_out(o, z, ng, wout)
            outs["s_s"].append(s_s)
            outs["c_s"].append(nbuf.reshape(GDN_CONV - 1, bs, GDN_QKV).transpose(1, 0, 2))
            ib += 1
        else:
            w = _nsa_weights(nsa_w_in[ic], nsa_gate_b[ic], nsa_cmp_pe[ic], nsa_cmp_w1[ic], nsa_cmp_w2[ic], nsa_w_out[ic])
            yp, kvn_p, winn_p = _nsa_prompt(xp, tp, w, t5_bias_table)
            past_len = page_table.shape[1] * cache_nsa_kv.shape[2]
            ys, kvn_s, winn_s = _nsa_decode(xs, ts, past_len, cache_nsa_kv[ic], cache_nsa_win[ic], page_table, w,
                                            t5_bias_table)
            outs["kv_p"].append(kvn_p)
            outs["win_p"].append(winn_p[:, tp - min(NSA_WINDOW, tp):])
            outs["kv_s"].append(kvn_s)
            outs["win_s"].append(winn_s)
            ic += 1
        wu, wd = bf(ffn_w_up[i, 1]), bf(ffn_w_down[i, 1])
        wg, wp = bf(ple_w_gate[i]), bf(ple_w_proj[i])
        pre = lambda y: (y, row(ln_g[i, 1]), row(ln_b[i, 1]))
        xp = _ffn(xp, wu, wd, row(ln_g[i, 2]), row(ln_b[i, 2]), pre=pre(yp), ple=(pp[i], wg, wp))
        xs = _ffn(xs, wu, wd, row(ln_g[i, 2]), row(ln_b[i, 2]), pre=pre(ys), ple=(ps[i], wg, wp))
    st = lambda k: jnp.stack(outs[k])
    return (xp.reshape(bp, tp, D_MODEL), xs.reshape(ts, bs, D_MODEL).transpose(1, 0, 2), st("a_v"), st("s_p"),
            st("c_p"), st("s_s"), st("c_s"), st("kv_p"), st("win_p"), st("kv_s"), st("win_s"))
```

```python
import functools
import math

import jax
import jax.numpy as jnp
from jax import lax
from jax.experimental import pallas as pl
from jax.experimental.pallas import tpu as pltpu

F32 = jnp.float32
BF16 = jnp.bfloat16
HIGHEST = lax.Precision.HIGHEST

D_MODEL = 1024
DEPTH = 4
ALPHA = (2 * DEPTH) ** 0.25
LN_EPS = 1e-5
NORM_EPS = 1e-6
D_FF = 2816
A_CHUNK = 128
A_INNER = 2 * D_MODEL
A_GROUPS = 16
GDN_H = 8
GDN_D = 128
GDN_QKV = 3 * D_MODEL
GDN_CONV = 4
GDN_CHUNK = 128
NSA_H = 16
NSA_DH = 64
NSA_G = 4
NSA_HG = 4
NSA_L_CMP = 32
NSA_L_SLC = 64
NSA_N_SEL = 16
NSA_WINDOW = 512
NSA_FORCE = 100.0
NEG_INF = -1e30
NUM_BUCKETS = 32
MAX_DISTANCE = 128

ROW_TILE = 256
ATT_TILE = 128
SW_TILE = 256
FAR_TILE = 512
VMEM_LIMIT = 56 << 20


def _cparams(*sem):
    return pltpu.CompilerParams(dimension_semantics=sem, vmem_limit_bytes=VMEM_LIMIT)


def _resident(shape):
    nd = len(shape)
    return pl.BlockSpec(shape, lambda *_: (0,) * nd, pipeline_mode=pl.Buffered(1))


def _rows(tm, width):
    return pl.BlockSpec((tm, width), lambda i: (i, 0))


def _bdot(a, b):
    return jnp.dot(a.astype(BF16), b.astype(BF16), preferred_element_type=F32)


def _bdot_nt(a, b):
    return lax.dot_general(a.astype(BF16), b.astype(BF16), (((1,), (1,)), ((), ())),
                           preferred_element_type=F32)


def _bdot_tn(a, b):
    return lax.dot_general(a.astype(BF16), b.astype(BF16), (((0,), (0,)), ((), ())),
                           preferred_element_type=F32)


def _hdot(a, b):
    return jnp.dot(a, b, precision=HIGHEST, preferred_element_type=F32)


def _split(a):
    hi = a.astype(BF16)
    return hi, (a - hi.astype(F32)).astype(BF16)


def _dot3(a, b):
    d = lambda x, y: jnp.dot(x, y, preferred_element_type=F32)
    return d(a[0], b[0]) + (d(a[0], b[1]) + d(a[1], b[0]))


def _sigmoid(x):
    return 1.0 / (1.0 + jnp.exp(-x))


def _silu(x):
    return x * _sigmoid(x)


def _gelu(x):
    return 0.5 * x * (1.0 + jnp.tanh(math.sqrt(2.0 / math.pi) * (x + 0.044715 * (x * x * x))))


def _softplus(x):
    return jnp.maximum(x, 0.0) + jnp.log(1.0 + jnp.exp(-jnp.abs(x)))


def _ln(y, g, b):
    mu = jnp.mean(y, axis=-1, keepdims=True)
    d = y - mu
    var = jnp.mean(d * d, axis=-1, keepdims=True)
    return d * lax.rsqrt(var + LN_EPS) * g + b


def _ffn_body(*refs, pre_norm, ple, chain):
    it = iter(refs)
    x_ref = next(it)
    if pre_norm:
        y_ref, g1, b1 = next(it), next(it), next(it)
    wu, wd, g2, b2 = next(it), next(it), next(it), next(it)
    if ple:
        p_ref, wg, wp = next(it), next(it), next(it)
    if chain:
        wu_n, wd_n, g_n, b_n = next(it), next(it), next(it), next(it)
    o_ref = next(it)

    def half_step(x, wu, wd, g, b):
        h = _bdot(x, wu[...])
        act = _silu(h[:, :D_FF]) * h[:, D_FF:]
        return _ln(ALPHA * x + 0.5 * _bdot(act, wd[...]), g[...], b[...])

    x = x_ref[...]
    if pre_norm:
        x = _ln(ALPHA * x + y_ref[...], g1[...], b1[...])
    x = half_step(x, wu, wd, g2, b2)
    if ple:
        x = x + _sigmoid(_bdot(x, wg[...])) * _bdot(p_ref[...], wp[...])
    if chain:
        x = half_step(x, wu_n, wd_n, g_n, b_n)
    o_ref[...] = x


def _ffn(x, wu, wd, g2, b2, pre=None, ple=None, chain=None):
    n = x.shape[0]
    tm = ROW_TILE
    args, specs = [x], [_rows(tm, D_MODEL)]
    if pre is not None:
        y, g1, b1 = pre
        args += [y, g1, b1]
        specs += [_rows(tm, D_MODEL), _resident((1, D_MODEL)), _resident((1, D_MODEL))]
    args += [wu, wd, g2, b2]
    specs += [_resident(wu.shape), _resident(wd.shape), _resident((1, D_MODEL)), _resident((1, D_MODEL))]
    if ple is not None:
        p, wg, wp = ple
        args += [p, wg, wp]
        specs += [_rows(tm, p.shape[1]), _resident(wg.shape), _resident(wp.shape)]
    if chain is not None:
        args += list(chain)
        specs += [_resident(a.shape) for a in chain]
    return pl.pallas_call(
        functools.partial(_ffn_body, pre_norm=pre is not None, ple=ple is not None, chain=chain is not None),
        grid=(n // tm,), in_specs=specs, out_specs=_rows(tm, D_MODEL),
        out_shape=jax.ShapeDtypeStruct((n, D_MODEL), F32),
        compiler_params=_cparams("parallel"), name="ffn")(*args)


def _mixa_body(x_ref, win, lg, lb, ws, bs, wout, y_ref, *rest, decode):
    if decode:
        v_ref, s_ref = rest
    else:
        (s_ref,) = rest
    tm = x_ref.shape[0]
    h = _gelu(_bdot(x_ref[...], win[...]))
    u = h[:, :A_INNER]
    v = _ln(h[:, A_INNER:], lg[...], lb[...])
    if decode:
        v_ref[...] = v
        nb = tm // 4
        for t in range(4):
            acc = bs[t:t + 1, :]
            for s in range(t + 1):
                acc = acc + ws[4 * t + s:4 * t + s + 1, :] * v[s * nb:(s + 1) * nb, :]
            s_ref[t * nb:(t + 1) * nb, :] = acc
    else:
        for c in range(tm // A_CHUNK):
            r = slice(c * A_CHUNK, (c + 1) * A_CHUNK)
            for g in range(A_GROUPS):
                l = slice(g * 128, (g + 1) * 128)
                s_ref[r, l] = jnp.dot(ws[g], v[r, l].astype(BF16), preferred_element_type=F32) + bs[:, l]
    y_ref[...] = _bdot(u * s_ref[...], wout[...])


def _mix_a(x, win, lg, lb, ws, bs, wout, decode):
    n = x.shape[0]
    tm = n if decode else ROW_TILE
    out_shape = [jax.ShapeDtypeStruct((n, D_MODEL), F32)]
    out_specs = [_rows(tm, D_MODEL)]
    if decode:
        out_shape.append(jax.ShapeDtypeStruct((n, A_INNER), F32))
        out_specs.append(_rows(tm, A_INNER))
    res = pl.pallas_call(
        functools.partial(_mixa_body, decode=decode),
        grid=(n // tm,),
        in_specs=[_rows(tm, D_MODEL), _resident(win.shape), _resident((1, A_INNER)), _resident((1, A_INNER)),
                  _resident(ws.shape), _resident(bs.shape), _resident(wout.shape)],
        out_specs=out_specs, out_shape=out_shape,
        scratch_shapes=[pltpu.VMEM((tm, A_INNER), F32)],
        compiler_params=_cparams("parallel"), name="mix_a_dec" if decode else "mix_a")(x, win, lg, lb, ws, bs, wout)
    return res


def _gdn_post(act, x, wz, wba, arow, dtrow, q_o, k_o, v_o, z_o, ba_o):
    for h in range(GDN_H):
        l = slice(h * GDN_D, (h + 1) * GDN_D)
        qh = act[:, l]
        q_o[:, l] = qh * lax.rsqrt(jnp.sum(qh * qh, axis=-1, keepdims=True) + NORM_EPS) * (GDN_D ** -0.5)
        kh = act[:, D_MODEL + h * GDN_D:D_MODEL + (h + 1) * GDN_D]
        k_o[:, l] = kh * lax.rsqrt(jnp.sum(kh * kh, axis=-1, keepdims=True) + NORM_EPS)
    v_o[...] = act[:, 2 * D_MODEL:]
    z_o[...] = _bdot(x, wz[...])
    lg = _bdot(x, wba[...])
    lane = lax.broadcasted_iota(jnp.int32, lg.shape, 1)
    ba_o[...] = jnp.where(lane < GDN_H, _sigmoid(lg), arow[...] * _softplus(lg + dtrow[...]))


def _gdn_pre_body(x_ref, xh_ref, wqkv, wz, wba, cw, arow, dtrow, q_o, k_o, v_o, z_o, ba_o, tail_o, ext,
                  *, tiles_per_seq):
    tm = x_ref.shape[0]
    x = x_ref[...]
    pre = _bdot(x, wqkv[...])
    halo = _bdot(xh_ref[...], wqkv[...])
    first = pl.program_id(0) % tiles_per_seq == 0
    ext[0:8, :] = jnp.where(first, 0.0, halo)
    ext[8:8 + tm, :] = pre
    tail_o[0] = pre[tm - 8:, :]
    conv = ext[pl.ds(5, tm), :] * cw[0:1, :]
    for j in range(1, GDN_CONV):
        conv = conv + ext[pl.ds(5 + j, tm), :] * cw[j:j + 1, :]
    _gdn_post(_silu(conv), x, wz, wba, arow, dtrow, q_o, k_o, v_o, z_o, ba_o)


def _gdn_pre_dec_body(x_ref, buf_ref, wqkv, wz, wba, cw, arow, dtrow, q_o, k_o, v_o, z_o, ba_o, nbuf_o, ext):
    n = x_ref.shape[0]
    nb = n // 4
    x = x_ref[...]
    ext[0:3 * nb, :] = buf_ref[...]
    ext[3 * nb:3 * nb + n, :] = _bdot(x, wqkv[...])
    nbuf_o[...] = ext[4 * nb:7 * nb, :]
    conv = ext[0:n, :] * cw[0:1, :]
    for j in range(1, GDN_CONV):
        conv = conv + ext[j * nb:j * nb + n, :] * cw[j:j + 1, :]
    _gdn_post(_silu(conv), x, wz, wba, arow, dtrow, q_o, k_o, v_o, z_o, ba_o)


def _gdn_pre(x, wqkv, wz, wba, cw, arow, dtrow, seq_len=None, conv_buf=None):
    n = x.shape[0]
    w_specs = [_resident(wqkv.shape), _resident(wz.shape), _resident(wba.shape), _resident(cw.shape),
               _resident((1, 128)), _resident((1, 128))]
    outs = [jax.ShapeDtypeStruct((n, D_MODEL), F32)] * 4 + [jax.ShapeDtypeStruct((n, 128), F32)]
    if conv_buf is None:
        tm = ROW_TILE
        tps = seq_len // tm
        nseq = n // seq_len
        return pl.pallas_call(
            functools.partial(_gdn_pre_body, tiles_per_seq=tps), grid=(n // tm,),
            in_specs=[_rows(tm, D_MODEL),
                      pl.BlockSpec((8, D_MODEL), lambda i: (jnp.maximum(i * (tm // 8) - 1, 0), 0))] + w_specs,
            out_specs=[_rows(tm, D_MODEL)] * 4 + [_rows(tm, 128),
                                                   pl.BlockSpec((1, 8, GDN_QKV), lambda i: (i // tps, 0, 0))],
            out_shape=outs + [jax.ShapeDtypeStruct((nseq, 8, GDN_QKV), F32)],
            scratch_shapes=[pltpu.VMEM((tm + 8, GDN_QKV), F32)],
            compiler_params=_cparams("arbitrary"), name="gdn_pre")(x, x, wqkv, wz, wba, cw, arow, dtrow)
    nb = n // 4
    return pl.pallas_call(
        _gdn_pre_dec_body, grid=(1,),
        in_specs=[_rows(n, D_MODEL), _rows(3 * nb, GDN_QKV)] + w_specs,
        out_specs=[_rows(n, D_MODEL)] * 4 + [_rows(n, 128), _rows(3 * nb, GDN_QKV)],
        out_shape=outs + [jax.ShapeDtypeStruct((3 * nb, GDN_QKV), F32)],
        scratch_shapes=[pltpu.VMEM((7 * nb, GDN_QKV), F32)],
        compiler_params=_cparams("arbitrary"), name="gdn_pre_dec")(x, conv_buf, wqkv, wz, wba, cw, arow, dtrow)


def _gdn_scan_body(q_ref, k_ref, v_ref, ba_ref, s0_ref, o_ref, s_ref):
    c = q_ref.shape[0]
    heads = range(GDN_H)

    @pl.when(pl.program_id(1) == 0)
    def _():
        s_ref[...] = s0_ref[...]

    ba = ba_ref[...]
    ri = lax.broadcasted_iota(jnp.int32, (c, c), 0)
    ci = lax.broadcasted_iota(jnp.int32, (c, c), 1)
    incl = ci <= ri
    strict = ci < ri
    eye = (ri == ci).astype(F32)
    gcum_all = _hdot(incl.astype(F32), ba)
    lanes = [slice(h * GDN_D, (h + 1) * GDN_D) for h in heads]
    gc, decay, kbeta, inv, pw = [], [], [], [], []
    for h in heads:
        kh = k_ref[:, lanes[h]]
        g = gcum_all[:, GDN_H + h:GDN_H + h + 1]
        g_row = jnp.sum(g * eye, axis=0, keepdims=True)
        d = jnp.where(incl, jnp.exp(jnp.where(incl, g - g_row, 0.0)), 0.0)
        kb = kh * ba[:, h:h + 1]
        a = jnp.where(strict, _bdot_nt(kb, kh) * d, 0.0)
        gc.append(g), decay.append(d), kbeta.append(kb), inv.append(eye - a), pw.append(a)
    pw = [_split(p) for p in pw]
    for _ in range(int(math.log2(c)) - 1):
        pw = [_split(_dot3(p, p)) for p in pw]
        inv = [t + _dot3(_split(t), p) for t, p in zip(inv, pw)]
    u = [_bdot(inv[h], v_ref[:, lanes[h]] * ba[:, h:h + 1]) for h in heads]
    w = [_bdot(inv[h], kbeta[h] * jnp.exp(gc[h])) for h in heads]
    qk = [jnp.where(incl, _bdot_nt(q_ref[:, lanes[h]], k_ref[:, lanes[h]]) * decay[h], 0.0) for h in heads]
    s = [s_ref[0, h] for h in heads]
    v_new = [u[h] - _bdot(w[h], s[h]) for h in heads]
    for h in heads:
        o_ref[:, lanes[h]] = _bdot(q_ref[:, lanes[h]] * jnp.exp(gc[h]), s[h]) + _bdot(qk[h], v_new[h])
    for h in heads:
        g_last = gc[h][c - 1:c, :]
        kd = k_ref[:, lanes[h]] * jnp.exp(g_last - gc[h])
        s_ref[0, h] = s[h] * jnp.exp(g_last) + _bdot_tn(kd, v_new[h])


def _gdn_scan(q, k, v, ba, s0, chunk):
    n = q.shape[0]
    nseq = s0.shape[0]
    nch = n // nseq // chunk
    row = lambda w: pl.BlockSpec((chunk, w), lambda b, c: (b * nch + c, 0))
    st = pl.BlockSpec((1, GDN_H, GDN_D, GDN_D), lambda b, c: (b, 0, 0, 0))
    return pl.pallas_call(
        _gdn_scan_body, grid=(nseq, nch),
        in_specs=[row(D_MODEL), row(D_MODEL), row(D_MODEL), row(128), st],
        out_specs=[row(D_MODEL), st],
        out_shape=[jax.ShapeDtypeStruct((n, D_MODEL), F32), jax.ShapeDtypeStruct(s0.shape, F32)],
        compiler_params=_cparams("parallel", "arbitrary"), name="gdn_scan")(q, k, v, ba, s0)


def _gdn_out_body(o_ref, z_ref, ng, wout, y_ref, t_ref):
    for h in range(GDN_H):
        l = slice(h * GDN_D, (h + 1) * GDN_D)
        oh = o_ref[:, l]
        t_ref[:, l] = oh * lax.rsqrt(jnp.mean(oh * oh, axis=-1, keepdims=True) + NORM_EPS)
    y_ref[...] = _bdot(t_ref[...] * ng[...] * _silu(z_ref[...]), wout[...])


def _gdn_out(o, z, ng, wout):
    n = o.shape[0]
    tm = ROW_TILE
    return pl.pallas_call(
        _gdn_out_body, grid=(n // tm,),
        in_specs=[_rows(tm, D_MODEL), _rows(tm, D_MODEL), _resident((1, D_MODEL)), _resident(wout.shape)],
        out_specs=_rows(tm, D_MODEL), out_shape=jax.ShapeDtypeStruct((n, D_MODEL), F32),
        scratch_shapes=[pltpu.VMEM((tm, D_MODEL), F32)],
        compiler_params=_cparams("parallel"), name="gdn_out")(o, z, ng, wout)


NSA_SCALE = NSA_DH ** -0.5
PAGE = 128


def _projt_body(x_ref, wt_ref, *o_refs):
    yt = _bdot_nt(wt_ref[...], x_ref[...])
    off = 0
    for o in o_refs:
        wdt = o.shape[1]
        o[0] = yt[off:off + wdt, :]
        off += wdt


def _projt(x, wt, widths, seq_len):
    n = x.shape[0]
    tm = ROW_TILE
    tps = seq_len // tm
    return pl.pallas_call(
        _projt_body, grid=(n // tm,),
        in_specs=[_rows(tm, D_MODEL), _resident(wt.shape)],
        out_specs=[pl.BlockSpec((1, wd, tm), lambda i: (i // tps, 0, i % tps)) for wd in widths],
        out_shape=[jax.ShapeDtypeStruct((n // seq_len, wd, seq_len), F32) for wd in widths],
        compiler_params=_cparams("parallel"), name="nsa_proj")(x, wt)


def _cmp_body(*refs, n_pages, group, paged):
    if paged:
        refs = refs[1:]
    srcs, (pet, perm, w1, w2, o_ref, flat) = refs[:-6], refs[-6:]
    n_blk = n_pages * (PAGE // NSA_L_CMP)
    slot = pl.program_id(0) % group
    half = NSA_L_CMP // 2
    for c in range(2):
        for gp in range(2):
            def slab(p):
                if paged:
                    return srcs[p][0, 0, c, 2 * gp:2 * gp + 2].reshape(2 * NSA_DH, PAGE) + pet[c]
                return srcs[0][0, (2 * c + gp) * 128:(2 * c + gp + 1) * 128, p * PAGE:(p + 1) * PAGE] + pet[c]

            for pp in range(n_pages // 2):
                x2 = jnp.concatenate([slab(2 * pp), slab(2 * pp + 1)], axis=1)
                y2 = _bdot(x2, perm[...])
                rows = pl.ds(pl.multiple_of(slot * n_blk + pp * 8, 8), 8)
                for hlf in range(2):
                    t = y2[:, hlf * PAGE:(hlf + 1) * PAGE].T
                    for l in range(half):
                        lane0 = (hlf * half + l) * 128
                        flat[2 * c + gp, rows, lane0:lane0 + 128] = t[l * 8:(l + 1) * 8, :]

    @pl.when(slot == group - 1)
    def _():
        for c in range(2):
            for gp in range(2):
                hid = _gelu(_bdot(flat[2 * c + gp], w1[c]))
                o_ref[c, :, gp * 128:(gp + 1) * 128] = _bdot(hid, w2[c])


def _compress(src, pet, perm, w1bd, w2bd, page_table=None):
    paged = page_table is not None
    if paged:
        nseq, n_pages = page_table.shape
        group = 4
        src_specs = [pl.BlockSpec((1, 1, 2, NSA_G, NSA_DH, PAGE),
                                  functools.partial(lambda b, pt, p: (0, pt[b, p], 0, 0, 0, 0), p=p))
                     for p in range(n_pages)]
        srcs = [src] * n_pages
        wmap = lambda nd: (lambda b, pt: (0,) * nd)
        omap = lambda b, pt: (0, b // group, 0)
    else:
        nseq, _, t = src.shape
        n_pages = t // PAGE
        group = 1
        src_specs = [pl.BlockSpec((1, 2 * NSA_G * NSA_DH, t), lambda b: (b, 0, 0))]
        srcs = [src]
        wmap = lambda nd: (lambda b: (0,) * nd)
        omap = lambda b: (0, b, 0)
    n_blk = n_pages * (PAGE // NSA_L_CMP)
    in_specs = src_specs + [pl.BlockSpec(a.shape, wmap(a.ndim), pipeline_mode=pl.Buffered(1))
                            for a in (pet, perm, w1bd, w2bd)]
    out_spec = pl.BlockSpec((2, group * n_blk, NSA_G * NSA_DH), omap)
    scratch = [pltpu.VMEM((4, group * n_blk, NSA_L_CMP * 128), F32)]
    out_shape = jax.ShapeDtypeStruct((2, nseq * n_blk, NSA_G * NSA_DH), F32)
    body = functools.partial(_cmp_body, n_pages=n_pages, group=group, paged=paged)
    if paged:
        gs = pltpu.PrefetchScalarGridSpec(num_scalar_prefetch=1, grid=(nseq,), in_specs=in_specs, out_specs=out_spec,
                                          scratch_shapes=scratch)
        return pl.pallas_call(body, grid_spec=gs, out_shape=out_shape, compiler_params=_cparams("arbitrary"),
                              name="nsa_compress_paged")(page_table, *srcs, pet, perm, w1bd, w2bd)
    return pl.pallas_call(body, grid=(nseq,), in_specs=in_specs, out_specs=out_spec, out_shape=out_shape,
                          scratch_shapes=scratch, compiler_params=_cparams("arbitrary"),
                          name="nsa_compress")(*srcs, pet, perm, w1bd, w2bd)


def _select(ps, jq, n_slc):
    n_p, w = ps.shape
    j = lax.broadcasted_iota(jnp.int32, (n_p, w), 0)
    forced = (j == 0) | (j == jq) | (j == jq - 1)
    sc = jnp.where(forced, NSA_FORCE, jnp.where(j > jq, -1.0, ps))
    sc = jnp.where(j >= n_slc, -2.0, sc)
    rank = jnp.zeros((n_p, w), F32)
    for i in range(n_slc):
        r = sc[i:i + 1, :]
        rank = rank + jnp.where((r > sc) | ((r == sc) & (j > i)), 1.0, 0.0)
    return jnp.where(rank < NSA_N_SEL, 1.0, 0.0)


def _softmax_keys(s, mask):
    m = jnp.max(s, axis=0, keepdims=True)
    e = jnp.exp(s - m)
    return jnp.where(mask, e / jnp.sum(e, axis=0, keepdims=True), 0.0)


def _q4(q_ref, g0, tq):
    return jnp.concatenate([q_ref[0, (g0 + hh) * NSA_DH:(g0 + hh + 1) * NSA_DH, :] for hh in range(NSA_HG)], axis=1)


def _cmp_attn_body(q_ref, kc_ref, vc_ref, b_ref, pair_ref, o_ref, sel_ref, *, n_slc):
    tq = q_ref.shape[-1]
    lanes = NSA_HG * tq
    kc, vc = kc_ref[0], vc_ref[0]
    qpos = pl.program_id(1) * tq + lax.broadcasted_iota(jnp.int32, (1, tq), 1)
    for g in range(NSA_G):
        f = slice(g * NSA_DH, (g + 1) * NSA_DH)
        bias = b_ref[0, :, g * lanes:(g + 1) * lanes]
        mask = bias > 0.5 * NEG_INF
        s = jnp.where(mask, _bdot(kc[:, f], _q4(q_ref, g * NSA_HG, tq)) * NSA_SCALE + bias, NEG_INF)
        p = _softmax_keys(s, mask)
        o = _bdot_tn(vc[:, f], p)
        ps = p[:, :tq]
        for hh in range(NSA_HG):
            o_ref[0, (g * NSA_HG + hh) * NSA_DH:(g * NSA_HG + hh + 1) * NSA_DH, :] = o[:, hh * tq:(hh + 1) * tq]
            if hh:
                ps = ps + p[:, hh * tq:(hh + 1) * tq]
        sel_ref[0, g] = _select(_hdot(pair_ref[...], ps), qpos // NSA_L_SLC, n_slc)


def _cmp_attn(qt, kc, vc, bias, pair, n_slc):
    b, _, t = qt.shape
    tq = ATT_TILE
    n_cmp = kc.shape[1]
    n_p = pair.shape[0]
    qs = pl.BlockSpec((1, D_MODEL, tq), lambda b_, i: (b_, 0, i))
    ks = pl.BlockSpec((1, n_cmp, NSA_G * NSA_DH), lambda b_, i: (b_, 0, 0))
    return pl.pallas_call(
        functools.partial(_cmp_attn_body, n_slc=n_slc), grid=(b, t // tq),
        in_specs=[qs, ks, ks, pl.BlockSpec((1, n_cmp, NSA_H * tq), lambda b_, i: (i, 0, 0)),
                  pl.BlockSpec(pair.shape, lambda b_, i: (0, 0))],
        out_specs=[qs, pl.BlockSpec((1, NSA_G, n_p, tq), lambda b_, i: (b_, 0, 0, i))],
        out_shape=[jax.ShapeDtypeStruct(qt.shape, F32), jax.ShapeDtypeStruct((b, NSA_G, n_p, t), F32)],
        compiler_params=_cparams("parallel", "parallel"), name="nsa_cmp_attn")(qt, kc, vc, bias, pair)


def _sw_attn_body(q_ref, ks_ref, vs_ref, sel_ref, b_ref, kw_ref, vw_ref, os_ref, ow_ref, negs, m_s, acc_s):
    tq = SW_TILE
    lanes = NSA_HG * tq
    n_tab = b_ref.shape[0]
    qt = pl.program_id(2)
    q0 = qt * tq
    q = (_q4(q_ref, 0, tq) * NSA_SCALE).astype(BF16)
    negs[...] = (jnp.concatenate([sel_ref[0, 0]] * NSA_HG, axis=1) - 1.0) * (-NEG_INF)
    ones = jnp.ones((8, n_tab), BF16)

    def sel_neg(k0, size):
        rows = [jnp.broadcast_to(negs[pl.ds(k0 // NSA_L_SLC + i, 1), :], (NSA_L_SLC, lanes))
                for i in range(size // NSA_L_SLC)]
        return jnp.concatenate(rows, axis=0)

    def scores(k_ref, k0, size, add):
        return _bdot_tn(k_ref[0, :, pl.ds(k0, size)], q) + add

    def update(s, v_ref, k0, size):
        vt = jnp.concatenate([v_ref[0, :, pl.ds(k0, size)].astype(BF16), ones[:, :size]], axis=0)
        m_old = m_s[...]
        m_new = jnp.maximum(m_old, jnp.max(s, axis=0, keepdims=True))
        p = jnp.exp(s - m_new).astype(BF16)
        acc_s[...] = jnp.exp(m_old - m_new) * acc_s[...] + jnp.dot(vt, p, preferred_element_type=F32)
        m_s[...] = m_new

    def near(k_ref, v_ref, size, masked):
        k0 = pl.multiple_of(q0 + tq - size, tq)
        add = b_ref[n_tab - size:, :]
        if masked:
            add = add + sel_neg(k0, size)
        update(scores(k_ref, k0, size, add), v_ref, k0, size)

    def init():
        m_s[...] = jnp.full(m_s.shape, NEG_INF, F32)
        acc_s[...] = jnp.zeros(acc_s.shape, F32)

    def write(o_ref):
        acc = acc_s[...]
        l = acc[NSA_DH:NSA_DH + 1, :]
        o = acc[:NSA_DH, :] / jnp.where(l > 0.0, l, 1.0)
        for hh in range(NSA_HG):
            o_ref[0, hh * NSA_DH:(hh + 1) * NSA_DH, :] = o[:, hh * tq:(hh + 1) * tq]

    init()
    pl.when(qt == 0)(lambda: near(ks_ref, vs_ref, tq, True))
    pl.when(qt >= 1)(lambda: near(ks_ref, vs_ref, 2 * tq, True))
    n_far = jnp.maximum(qt - 1, 0) * tq
    n_big = n_far // FAR_TILE

    def far_pair(i, carry):
        ka = pl.multiple_of(2 * i * FAR_TILE, FAR_TILE)
        kb = pl.multiple_of((2 * i + 1) * FAR_TILE, FAR_TILE)
        sa = scores(ks_ref, ka, FAR_TILE, sel_neg(ka, FAR_TILE))
        sb = scores(ks_ref, kb, FAR_TILE, sel_neg(kb, FAR_TILE))
        update(sa, vs_ref, ka, FAR_TILE)
        update(sb, vs_ref, kb, FAR_TILE)
        return carry

    lax.fori_loop(0, n_big // 2, far_pair, 0)

    def far_one(k0, size):
        update(scores(ks_ref, k0, size, sel_neg(k0, size)), vs_ref, k0, size)

    pl.when(n_big % 2 == 1)(lambda: far_one(pl.multiple_of((n_big - 1) * FAR_TILE, FAR_TILE), FAR_TILE))
    pl.when(n_far % FAR_TILE != 0)(lambda: far_one(pl.multiple_of(n_big * FAR_TILE, tq), tq))
    write(os_ref)

    init()
    for j in range(1, n_tab // tq):
        pl.when(qt == j - 1)(functools.partial(near, kw_ref, vw_ref, j * tq, False))
    pl.when(qt >= n_tab // tq - 1)(lambda: near(kw_ref, vw_ref, n_tab, False))
    write(ow_ref)


def _sw_attn(qt, kvt, wint, sel, btab):
    b, _, t = qt.shape
    tq = SW_TILE
    lanes = NSA_HG * tq
    qs = pl.BlockSpec((1, NSA_HG * NSA_DH, tq), lambda b_, g_, i: (b_, g_, i))
    feat = lambda blk0: pl.BlockSpec((1, NSA_DH, t), lambda b_, g_, i: (b_, blk0 + g_, 0))
    return pl.pallas_call(
        _sw_attn_body, grid=(b, NSA_G, t // tq),
        in_specs=[qs, feat(2 * NSA_G), feat(3 * NSA_G),
                  pl.BlockSpec((1, 1, sel.shape[2], tq), lambda b_, g_, i: (b_, g_, 0, i)),
                  pl.BlockSpec((btab.shape[0], lanes), lambda b_, g_, i: (0, g_)),
                  feat(0), feat(NSA_G)],
        out_specs=[qs, qs],
        out_shape=[jax.ShapeDtypeStruct(qt.shape, F32)] * 2,
        scratch_shapes=[pltpu.VMEM((sel.shape[2], lanes), F32), pltpu.VMEM((1, lanes), F32),
                        pltpu.VMEM((NSA_DH + 8, lanes), F32)],
        compiler_params=_cparams("parallel", "parallel", "arbitrary"),
        name="nsa_sw_attn")(qt, kvt, kvt, sel, btab, wint, wint)


def _attend(q, kts, vts, adds):
    s = [_bdot_tn(kt, q) + add for kt, add in zip(kts, adds)]
    m = functools.reduce(jnp.maximum, [jnp.max(x, axis=0, keepdims=True) for x in s])
    p = [jnp.exp(x - m) for x in s]
    l = functools.reduce(jnp.add, [jnp.sum(x, axis=0, keepdims=True) for x in p])
    acc = functools.reduce(jnp.add, [_bdot(vt, x) for vt, x in zip(vts, p)])
    return acc / l


def _dec_attn_body(pt_ref, *refs, n_pages, n_slc, q_pos0, tq):
    pages = refs[:n_pages]
    (q_ref, kc_ref, vc_ref, bc_ref, pair_ref, hsum_ref, knew_ref, vnew_ref, et_ref, bs_ref, win_ref, kwnew_ref,
     vwnew_ref, bw_ref, oc_ref, os_ref, ow_ref) = refs[n_pages:]
    feats, lanes = q_ref.shape[1:]
    q = (q_ref[0] * NSA_SCALE).astype(BF16)

    bias = bc_ref[...]
    mask = bias > 0.5 * NEG_INF
    p = _softmax_keys(jnp.where(mask, _bdot(kc_ref[0], q) + bias, NEG_INF), mask)
    oc_ref[0] = _bdot_tn(vc_ref[0], p)
    ps = _hdot(pair_ref[...], _hdot(p, hsum_ref[...]))
    lane = lax.broadcasted_iota(jnp.int32, (1, lanes), 1)
    sel = _select(ps, (q_pos0 + lane % tq) // NSA_L_SLC, n_slc).astype(BF16)

    kts = [pages[j][0, 0, 0].reshape(feats, PAGE) for j in range(n_pages)] + [knew_ref[0]]
    vts = [pages[j][0, 0, 1].reshape(feats, PAGE) for j in range(n_pages)] + [vnew_ref[0]]
    adds = [bs_ref[j] + (jnp.dot(et_ref[j], sel, preferred_element_type=F32) - 1.0) * (-NEG_INF)
            for j in range(n_pages + 1)]
    os_ref[0] = _attend(q, kts, vts, adds)

    n_wt = win_ref.shape[-1] // PAGE
    kts = [win_ref[0, 0, :, j * PAGE:(j + 1) * PAGE] for j in range(n_wt)] + [kwnew_ref[0]]
    vts = [win_ref[0, 1, :, j * PAGE:(j + 1) * PAGE] for j in range(n_wt)] + [vwnew_ref[0]]
    ow_ref[0] = _attend(q, kts, vts, [bw_ref[j] for j in range(n_wt + 1)])


def _dec_attn(page_table, cache_t, qbd, kc, vc, bias_c, pair, hsum, knew, vnew, et, bsel, win_t, kwnew, vwnew, bwin,
              n_slc, q_pos0, tq):
    nseq, n_pages = page_table.shape
    per_seq = lambda a: pl.BlockSpec((1,) + a.shape[1:], lambda b, pt: (b,) + (0,) * (a.ndim - 1))
    whole = lambda a: pl.BlockSpec(a.shape, lambda b, pt: (0,) * a.ndim, pipeline_mode=pl.Buffered(1))
    page_specs = [pl.BlockSpec((1, 1, 2, NSA_G, NSA_DH, PAGE),
                               functools.partial(lambda b, pt, p: (0, pt[b, p], 1, 0, 0, 0), p=p))
                  for p in range(n_pages)]
    gs = pltpu.PrefetchScalarGridSpec(
        num_scalar_prefetch=1, grid=(nseq,),
        in_specs=page_specs + [per_seq(qbd), per_seq(kc), per_seq(vc), whole(bias_c), whole(pair), whole(hsum),
                               per_seq(knew), per_seq(vnew), whole(et), whole(bsel), per_seq(win_t), per_seq(kwnew),
                               per_seq(vwnew), whole(bwin)],
        out_specs=[per_seq(qbd)] * 3)
    return pl.pallas_call(
        functools.partial(_dec_attn_body, n_pages=n_pages, n_slc=n_slc, q_pos0=q_pos0, tq=tq), grid_spec=gs,
        out_shape=[jax.ShapeDtypeStruct(qbd.shape, F32)] * 3, compiler_params=_cparams("parallel"),
        name="nsa_dec_attn")(page_table, *([cache_t] * n_pages), qbd, kc, vc, bias_c, pair, hsum, knew, vnew, et, bsel,
                             win_t, kwnew, vwnew, bwin)


def _nsa_out_body(x_ref, oc_ref, os_ref, ow_ref, wg, gb, wout, y_ref):
    gates = _sigmoid(_bdot_nt(wg[...], x_ref[...]) + gb[...])
    parts = []
    for h in range(NSA_H):
        r = slice(h * NSA_DH, (h + 1) * NSA_DH)
        parts.append(gates[h:h + 1, :] * oc_ref[0, r, :] + gates[NSA_H + h:NSA_H + h + 1, :] * os_ref[0, r, :]
                     + gates[2 * NSA_H + h:2 * NSA_H + h + 1, :] * ow_ref[0, r, :])
    y_ref[...] = _bdot_tn(jnp.concatenate(parts, axis=0), wout[...])


def _nsa_out(x, oc, osl, ow, wg, gb, wout):
    n = x.shape[0]
    tm = ROW_TILE
    tps = oc.shape[2] // tm
    col = pl.BlockSpec((1, D_MODEL, tm), lambda i: (i // tps, 0, i % tps))
    return pl.pallas_call(
        _nsa_out_body, grid=(n // tm,),
        in_specs=[_rows(tm, D_MODEL), col, col, col, _resident(wg.shape), _resident(gb.shape), _resident(wout.shape)],
        out_specs=_rows(tm, D_MODEL), out_shape=jax.ShapeDtypeStruct((n, D_MODEL), F32),
        compiler_params=_cparams("parallel"), name="nsa_out")(x, oc, osl, ow, wg, gb, wout)


def _t5_bucket(dist):
    n = jnp.maximum(dist, 0)
    max_exact = NUM_BUCKETS // 2
    nf = jnp.maximum(n, 1).astype(F32)
    large = max_exact + (jnp.log(nf / max_exact) / math.log(MAX_DISTANCE / max_exact)
                         * (NUM_BUCKETS - max_exact)).astype(jnp.int32)
    return jnp.where(n < max_exact, n, jnp.minimum(large, NUM_BUCKETS - 1))


def _bias_table(tab, dist, valid):
    onehot = (_t5_bucket(dist)[..., None] == jnp.arange(NUM_BUCKETS)).astype(F32)
    bias = jnp.einsum("...k,kh->...h", onehot, tab.astype(F32), precision=HIGHEST)
    bias = jnp.swapaxes(jnp.where(valid[..., None], bias, NEG_INF), -1, -2)
    return bias.reshape(bias.shape[:-2] + (NSA_H * dist.shape[-1],))


def _nsa_weights(w_in, gate_b, pe, w1, w2, w_out):
    n_att = NSA_H * NSA_DH + 6 * NSA_G * NSA_DH
    eye2 = jnp.eye(2, dtype=F32)
    w1r = w1.reshape(2, NSA_L_CMP, NSA_DH, -1)
    w1bd = jnp.einsum("cldj,pq->clpdqj", w1r, eye2).reshape(2, NSA_L_CMP * 2 * NSA_DH, 2 * w1.shape[-1])
    w2bd = jnp.einsum("cjd,pq->cpjqd", w2, eye2).reshape(2, 2 * w2.shape[1], 2 * NSA_DH)
    pet = jnp.broadcast_to(pe.transpose(0, 2, 1)[:, None, :, None, :],
                           (2, 2, NSA_DH, PAGE // NSA_L_CMP, NSA_L_CMP)).reshape(2, 2 * NSA_DH, PAGE)
    pos = jnp.arange(2 * PAGE)
    dest = (pos % NSA_L_CMP) * (2 * PAGE // NSA_L_CMP) + pos // NSA_L_CMP
    perm = (dest[:, None] == pos[None, :]).astype(BF16)
    return dict(perm=perm, wt_in=w_in[:, :n_att].T.astype(BF16), w_gate=w_in[:, n_att:].T.astype(BF16),
                gate_b=gate_b.reshape(-1, 1), pet=pet, w1bd=w1bd.astype(BF16), w2bd=w2bd.astype(BF16),
                w_out=w_out.astype(BF16))


_NSA_WIDTHS = (NSA_H * NSA_DH, 4 * NSA_G * NSA_DH, 2 * NSA_G * NSA_DH)


def _pair_matrix(n_cmp, n_slc_p):
    return (jnp.arange(n_cmp)[None, :] // (NSA_L_SLC // NSA_L_CMP) == jnp.arange(n_slc_p)[:, None]).astype(F32)


def _block_expand(n_tiles, n_slc_p):
    et = jnp.arange(n_tiles * ATT_TILE)[:, None] // NSA_L_SLC == jnp.arange(n_slc_p)[None, :]
    return et.astype(BF16).reshape(n_tiles, ATT_TILE, n_slc_p)


def _nsa_prompt(x, seq_len, w, tab):
    nseq = x.shape[0] // seq_len
    tq = ATT_TILE
    nq = seq_len // tq
    n_cmp = seq_len // NSA_L_CMP
    n_slc = -(-seq_len // NSA_L_SLC)
    n_slc_p = -(-n_slc // 8) * 8
    qt, kvt, wint = _projt(x, w["wt_in"], _NSA_WIDTHS, seq_len)
    cmp = _compress(kvt, w["pet"], w["perm"], w["w1bd"], w["w2bd"]).reshape(2, nseq, n_cmp, NSA_G * NSA_DH)

    qpos = (jnp.arange(nq, dtype=jnp.int32) * tq)[:, None, None] + jnp.arange(tq, dtype=jnp.int32)[None, None, :]
    dist = qpos - (jnp.arange(n_cmp, dtype=jnp.int32) * NSA_L_CMP + NSA_L_CMP - 1)[None, :, None]
    oc, sel = _cmp_attn(qt, cmp[0], cmp[1], _bias_table(tab, dist, dist >= 0), _pair_matrix(n_cmp, n_slc_p), n_slc)

    n_tab = NSA_WINDOW + SW_TILE
    d_t = (n_tab - SW_TILE + jnp.arange(SW_TILE, dtype=jnp.int32))[None, :] - jnp.arange(n_tab, dtype=jnp.int32)[:, None]
    btab = _bias_table(tab, d_t, (d_t >= 0) & (d_t < NSA_WINDOW))
    bfar = _bias_table(tab, jnp.full((1, SW_TILE), 4 * MAX_DISTANCE, jnp.int32), jnp.ones((1, SW_TILE), bool))
    osl, ow = _sw_attn(qt, kvt, wint, sel, jnp.where(btab > 0.5 * NEG_INF, btab - bfar, NEG_INF))
    y = _nsa_out(x, oc, osl, ow, w["w_gate"], w["gate_b"], w["w_out"])
    kv_new = kvt.reshape(nseq, 4, NSA_G, NSA_DH, seq_len).transpose(0, 4, 1, 2, 3)
    win_new = wint.reshape(nseq, 2, NSA_G, NSA_DH, seq_len).transpose(0, 4, 1, 2, 3)
    return y, kv_new, win_new


def _nsa_decode(x, n_tok, q_pos0, cache, win_past, page_table, w, tab):
    n = x.shape[0]
    nseq = n // n_tok
    tq = 8
    lanes = NSA_H * tq
    feats = NSA_G * NSA_DH
    n_pages = page_table.shape[1]
    past = n_pages * PAGE
    n_cmp = (past + n_tok) // NSA_L_CMP
    n_slc = -(-(past + n_tok) // NSA_L_SLC)
    n_slc_p = -(-n_slc // 8) * 8
    cache_t = cache.transpose(0, 2, 3, 4, 1)[None]
    win_t = win_past.transpose(0, 2, 3, 4, 1).reshape(nseq, 2, feats, win_past.shape[1])
    qt, kvt, wint = _projt(x, w["wt_in"], _NSA_WIDTHS, n)

    cmp = _compress(cache_t, w["pet"], w["perm"], w["w1bd"], w["w2bd"], page_table).reshape(2, nseq, n_cmp, feats)

    qg = qt[0].reshape(NSA_G, NSA_HG, NSA_DH, n_tok, nseq).transpose(4, 0, 2, 1, 3)
    qg = jnp.pad(qg, ((0, 0),) * 4 + ((0, tq - n_tok),)).reshape(nseq, NSA_G, NSA_DH, NSA_HG * tq)
    qbd = jnp.einsum("bgdl,gh->bgdhl", qg, jnp.eye(NSA_G, dtype=F32)).reshape(nseq, feats, lanes)
    new_t = lambda a: jnp.pad(a.reshape(feats, n_tok, nseq).transpose(2, 0, 1), ((0, 0), (0, 0), (0, PAGE - n_tok)))
    kv_new = kvt[0].reshape(4, NSA_G, NSA_DH, n_tok, nseq)
    win_new = wint[0].reshape(2, NSA_G, NSA_DH, n_tok, nseq)

    lane = jnp.arange(lanes, dtype=jnp.int32)
    qpos = q_pos0 + jnp.arange(tq, dtype=jnp.int32)[None, None, :]

    def bias_of(kpos, extra):
        d = qpos - kpos[:, :, None]
        return _bias_table(tab, d, (d >= 0) & extra(d, kpos[:, :, None]))

    tile_pos = lambda nt: (jnp.arange(nt, dtype=jnp.int32) * PAGE)[:, None] + jnp.arange(PAGE, dtype=jnp.int32)[None, :]
    kend = (jnp.arange(n_cmp, dtype=jnp.int32) * NSA_L_CMP + NSA_L_CMP - 1)[None, :]
    bias_c = bias_of(kend, lambda d, k: True)[0]
    bsel = bias_of(tile_pos(n_pages + 1), lambda d, k: k < past + n_tok)
    n_wt = win_past.shape[1] // PAGE
    wpos = jnp.concatenate([q_pos0 - n_wt * PAGE + tile_pos(n_wt), q_pos0 + tile_pos(1)])
    bwin = bias_of(wpos, lambda d, k: (d < NSA_WINDOW) & (k >= 0) & (k < q_pos0 + n_tok))
    hsum = ((lane[:, None] // (NSA_HG * tq) == lane[None, :] // (NSA_HG * tq))
            & (lane[:, None] % tq == lane[None, :] % tq)).astype(F32)

    outs = _dec_attn(page_table, cache_t, qbd, cmp[0], cmp[1], bias_c, _pair_matrix(n_cmp, n_slc_p), hsum,
                     new_t(kv_new[2]), new_t(kv_new[3]), _block_expand(n_pages + 1, n_slc_p), bsel, win_t,
                     new_t(win_new[0]), new_t(win_new[1]), bwin, n_slc, q_pos0, tq)

    def own_group(o):
        o = o.reshape(nseq, NSA_G, NSA_DH, NSA_G, NSA_HG, tq)
        o = jnp.diagonal(o, axis1=1, axis2=3)[:, :, :, :n_tok]
        return o.transpose(4, 2, 1, 3, 0).reshape(1, D_MODEL, n)

    y = _nsa_out(x, *(own_group(o) for o in outs), w["w_gate"], w["gate_b"], w["w_out"])
    return y, kv_new.transpose(4, 3, 0, 1, 2), win_new.transpose(4, 3, 0, 1, 2)


def kernel(x_prompt, x_sample, state_gdn_s, state_gdn_conv, cache_nsa_kv, cache_nsa_win, page_table, p_prompt,
           p_sample, ln_g, ln_b, ffn_w_up, ffn_w_down, ple_w_gate, ple_w_proj, a_w_in, a_ln_g, a_ln_b, a_w_s, a_b_s,
           a_w_out, gdn_w_in, gdn_conv_w, gdn_a_log, gdn_dt_bias, gdn_norm_g, gdn_w_out, nsa_w_in, nsa_gate_b,
           nsa_cmp_pe, nsa_cmp_w1, nsa_cmp_w2, nsa_w_out, t5_bias_table):
    bp, tp, _ = x_prompt.shape
    bs, ts, _ = x_sample.shape
    bf = lambda a: a.astype(BF16)
    row = lambda a: a.reshape(1, -1)
    xp = x_prompt.reshape(bp * tp, D_MODEL)
    xs = x_sample.transpose(1, 0, 2).reshape(ts * bs, D_MODEL)
    pp = p_prompt.reshape(DEPTH, bp * tp, -1)
    ps = p_sample.transpose(0, 2, 1, 3).reshape(DEPTH, ts * bs, -1)
    outs = {k: [] for k in ("a_v", "s_p", "c_p", "s_s", "c_s", "kv_p", "win_p", "kv_s", "win_s")}
    ia = ib = ic = 0
    ffn1 = lambda i: (bf(ffn_w_up[i, 0]), bf(ffn_w_down[i, 0]), row(ln_g[i, 0]), row(ln_b[i, 0]))
    xp, xs = _ffn(xp, *ffn1(0)), _ffn(xs, *ffn1(0))
    for i in range(DEPTH):
        kind = i % 3
        if kind == 0:
            causal = jnp.tril(jnp.ones((A_CHUNK, A_CHUNK), bool))
            ws = bf(jnp.where(causal, a_w_s[ia], 0.0))
            bexp = jnp.repeat(a_b_s[ia].T, 128, axis=1)
            win, wout = bf(a_w_in[ia]), bf(a_w_out[ia])
            lg, lb = row(a_ln_g[ia]), row(a_ln_b[ia])
            (yp,) = _mix_a(xp, win, lg, lb, ws, bexp, wout, decode=False)
            ws_dec = jnp.repeat(a_w_s[ia][:, :ts, :ts].transpose(1, 2, 0).reshape(ts * ts, A_GROUPS), 128, axis=1)
            ys, v_rows = _mix_a(xs, win, lg, lb, ws_dec, bexp[:8], wout, decode=True)
            outs["a_v"].append(v_rows.reshape(ts, bs, A_INNER).transpose(1, 0, 2))
            ia += 1
        elif kind == 1:
            w = gdn_w_in[ib]
            wqkv, wz = bf(w[:, :GDN_QKV]), bf(w[:, GDN_QKV:GDN_QKV + D_MODEL])
            wba = bf(jnp.pad(w[:, GDN_QKV + D_MODEL:], ((0, 0), (0, 128 - 2 * GDN_H))))
            arow = jnp.pad(-jnp.exp(gdn_a_log[ib]), (GDN_H, 128 - 2 * GDN_H)).reshape(1, 128)
            dtrow = jnp.pad(gdn_dt_bias[ib], (GDN_H, 128 - 2 * GDN_H)).reshape(1, 128)
            ng, wout = row(jnp.tile(gdn_norm_g[ib], GDN_H)), bf(gdn_w_out[ib])
            q, k, v, z, ba, tail = _gdn_pre(xp, wqkv, wz, wba, gdn_conv_w[ib], arow, dtrow, seq_len=tp)
            o, s_p = _gdn_scan(q, k, v, ba, jnp.zeros((bp, GDN_H, GDN_D, GDN_D), F32), GDN_CHUNK)
            yp = _gdn_out(o, z, ng, wout)
            outs["s_p"].append(s_p)
            outs["c_p"].append(tail[:, 8 - (GDN_CONV - 1):])
            buf_t = state_gdn_conv[ib].transpose(1, 0, 2).reshape((GDN_CONV - 1) * bs, GDN_QKV)
            q, k, v, z, ba, nbuf = _gdn_pre(xs, wqkv, wz, wba, gdn_conv_w[ib], arow, dtrow, conv_buf=buf_t)
            seq8 = lambda a: jnp.pad(a.reshape(ts, bs, -1).transpose(1, 0, 2), ((0, 0), (0, 8 - ts), (0, 0))).reshape(
                bs * 8, -1)
            o, s_s = _gdn_scan(seq8(q), seq8(k), seq8(v), seq8(ba), state_gdn_s[ib], 8)
            o = o.reshape(bs, 8, D_MODEL)[:, :ts].transpose(1, 0, 2).reshape(ts * bs, D_MODEL)
            ys = _gdn_out(o, z, ng, wout)
            outs["s_s"].append(s_s)
            outs["c_s"].append(nbuf.reshape(GDN_CONV - 1, bs, GDN_QKV).transpose(1, 0, 2))
            ib += 1
        else:
            w = _nsa_weights(nsa_w_in[ic], nsa_gate_b[ic], nsa_cmp_pe[ic], nsa_cmp_w1[ic], nsa_cmp_w2[ic], nsa_w_out[ic])
            yp, kvn_p, winn_p = _nsa_prompt(xp, tp, w, t5_bias_table)
            past_len = page_table.shape[1] * cache_nsa_kv.shape[2]
            ys, kvn_s, winn_s = _nsa_decode(xs, ts, past_len, cache_nsa_kv[ic], cache_nsa_win[ic], page_table, w,
                                            t5_bias_table)
            outs["kv_p"].append(kvn_p)
            outs["win_p"].append(winn_p[:, tp - min(NSA_WINDOW, tp):])
            outs["kv_s"].append(kvn_s)
            outs["win_s"].append(winn_s)
            ic += 1
        wu, wd = bf(ffn_w_up[i, 1]), bf(ffn_w_down[i, 1])
        wg, wp = bf(ple_w_gate[i]), bf(ple_w_proj[i])
        pre = lambda y: (y, row(ln_g[i, 1]), row(ln_b[i, 1]))
        chain = ffn1(i + 1) if i + 1 < DEPTH else None
        xp = _ffn(xp, wu, wd, row(ln_g[i, 2]), row(ln_b[i, 2]), pre=pre(yp), ple=(pp[i], wg, wp), chain=chain)
        xs = _ffn(xs, wu, wd, row(ln_g[i, 2]), row(ln_b[i, 2]), pre=pre(ys), ple=(ps[i], wg, wp), chain=chain)
    st = lambda k: jnp.stack(outs[k])
    return (xp.reshape(bp, tp, D_MODEL), xs.reshape(ts, bs, D_MODEL).transpose(1, 0, 2), st("a_v"), st("s_p"),
            st("c_p"), st("s_s"), st("c_s"), st("kv_p"), st("win_p"), st("kv_s"), st("win_s"))
```

```python
import functools
import math

import jax
import jax.numpy as jnp
from jax import lax
from jax.experimental import pallas as pl
from jax.experimental.pallas import tpu as pltpu

F32 = jnp.float32
BF16 = jnp.bfloat16
HIGHEST = lax.Precision.HIGHEST

D_MODEL = 1024
DEPTH = 4
ALPHA = (2 * DEPTH) ** 0.25
LN_EPS = 1e-5
NORM_EPS = 1e-6
D_FF = 2816
A_CHUNK = 128
A_INNER = 2 * D_MODEL
A_GROUPS = 16
GDN_H = 8
GDN_D = 128
GDN_QKV = 3 * D_MODEL
GDN_CONV = 4
GDN_CHUNK = 128
NSA_H = 16
NSA_DH = 64
NSA_G = 4
NSA_HG = 4
NSA_L_CMP = 32
NSA_L_SLC = 64
NSA_N_SEL = 16
NSA_WINDOW = 512
NSA_FORCE = 100.0
NEG_INF = -1e30
NUM_BUCKETS = 32
MAX_DISTANCE = 128

ROW_TILE = 256
ATT_TILE = 128
SW_TILE = 256
FAR_TILE = 512
VMEM_LIMIT = 56 << 20


def _cparams(*sem):
    return pltpu.CompilerParams(dimension_semantics=sem, vmem_limit_bytes=VMEM_LIMIT)


def _resident(shape):
    nd = len(shape)
    return pl.BlockSpec(shape, lambda *_: (0,) * nd, pipeline_mode=pl.Buffered(1))


def _rows(tm, width):
    return pl.BlockSpec((tm, width), lambda i: (i, 0))


def _layer(w):
    arr, idx = w if isinstance(w, tuple) else (w, ())
    nd = arr.ndim - len(idx)
    return arr, pl.BlockSpec((None,) * len(idx) + arr.shape[len(idx):], lambda *_: tuple(idx) + (0,) * nd,
                             pipeline_mode=pl.Buffered(1))


def _bdot(a, b):
    return jnp.dot(a.astype(BF16), b.astype(BF16), preferred_element_type=F32)


def _bdot_nt(a, b):
    return lax.dot_general(a.astype(BF16), b.astype(BF16), (((1,), (1,)), ((), ())),
                           preferred_element_type=F32)


def _bdot_tn(a, b):
    return lax.dot_general(a.astype(BF16), b.astype(BF16), (((0,), (0,)), ((), ())),
                           preferred_element_type=F32)


def _hdot(a, b):
    return jnp.dot(a, b, precision=HIGHEST, preferred_element_type=F32)


def _split(a):
    hi = a.astype(BF16)
    return hi, (a - hi.astype(F32)).astype(BF16)


def _dot3(a, b):
    d = lambda x, y: jnp.dot(x, y, preferred_element_type=F32)
    return d(a[0], b[0]) + (d(a[0], b[1]) + d(a[1], b[0]))


def _sigmoid(x):
    return 1.0 / (1.0 + jnp.exp(-x))


def _silu(x):
    return x * _sigmoid(x)


def _gelu(x):
    return 0.5 * x * (1.0 + jnp.tanh(math.sqrt(2.0 / math.pi) * (x + 0.044715 * (x * x * x))))


def _softplus(x):
    return jnp.maximum(x, 0.0) + jnp.log(1.0 + jnp.exp(-jnp.abs(x)))


def _ln(y, g, b):
    mu = jnp.mean(y, axis=-1, keepdims=True)
    d = y - mu
    var = jnp.mean(d * d, axis=-1, keepdims=True)
    return d * lax.rsqrt(var + LN_EPS) * g + b


def _ffn_body(*refs, pre_norm, ple, chain):
    it = iter(refs)
    x_ref = next(it)
    if pre_norm:
        y_ref, g1, b1 = next(it), next(it), next(it)
    wu, wd, g2, b2 = next(it), next(it), next(it), next(it)
    if ple:
        p_ref, wg, wp = next(it), next(it), next(it)
    if chain:
        wu_n, wd_n, g_n, b_n = next(it), next(it), next(it), next(it)
    o_ref = next(it)

    def half_step(x, wu, wd, g, b):
        h = _bdot(x, wu[...])
        act = _silu(h[:, :D_FF]) * h[:, D_FF:]
        return _ln(ALPHA * x + 0.5 * _bdot(act, wd[...]), g[...], b[...])

    x = x_ref[...]
    if pre_norm:
        x = _ln(ALPHA * x + y_ref[...], g1[...], b1[...])
    x = half_step(x, wu, wd, g2, b2)
    if ple:
        x = x + _sigmoid(_bdot(x, wg[...])) * _bdot(p_ref[...], wp[...])
    if chain:
        x = half_step(x, wu_n, wd_n, g_n, b_n)
    o_ref[...] = x


def _ffn(x, wu, wd, g2, b2, pre=None, ple=None, chain=None):
    n = x.shape[0]
    tm = ROW_TILE
    args, specs = [x], [_rows(tm, D_MODEL)]

    def resident(*ws):
        for w in ws:
            arr, spec = _layer(w)
            args.append(arr)
            specs.append(spec)

    if pre is not None:
        args.append(pre[0])
        specs.append(_rows(tm, D_MODEL))
        resident(*pre[1:])
    resident(wu, wd, g2, b2)
    if ple is not None:
        p, idx = ple[0] if isinstance(ple[0], tuple) else (ple[0], ())
        args.append(p)
        specs.append(pl.BlockSpec((None,) * len(idx) + (tm, p.shape[-1]), lambda i: tuple(idx) + (i, 0)))
        resident(*ple[1:])
    if chain is not None:
        resident(*chain)
    return pl.pallas_call(
        functools.partial(_ffn_body, pre_norm=pre is not None, ple=ple is not None, chain=chain is not None),
        grid=(n // tm,), in_specs=specs, out_specs=_rows(tm, D_MODEL),
        out_shape=jax.ShapeDtypeStruct((n, D_MODEL), F32),
        compiler_params=_cparams("parallel"), name="ffn")(*args)


def _mixa_body(x_ref, win, lg, lb, ws, bs, wout, y_ref, *rest, decode):
    if decode:
        v_ref, s_ref = rest
    else:
        (s_ref,) = rest
    tm = x_ref.shape[0]
    h = _gelu(_bdot(x_ref[...], win[...]))
    u = h[:, :A_INNER]
    v = _ln(h[:, A_INNER:], lg[...], lb[...])
    if decode:
        v_ref[...] = v
        nb = tm // 4
        for t in range(4):
            acc = bs[t:t + 1, :]
            for s in range(t + 1):
                acc = acc + ws[4 * t + s:4 * t + s + 1, :] * v[s * nb:(s + 1) * nb, :]
            s_ref[t * nb:(t + 1) * nb, :] = acc
    else:
        for c in range(tm // A_CHUNK):
            r = slice(c * A_CHUNK, (c + 1) * A_CHUNK)
            for g in range(A_GROUPS):
                l = slice(g * 128, (g + 1) * 128)
                s_ref[r, l] = jnp.dot(ws[g], v[r, l].astype(BF16), preferred_element_type=F32) + bs[:, l]
    y_ref[...] = _bdot(u * s_ref[...], wout[...])


def _mix_a(x, win, lg, lb, ws, bs, wout, decode):
    n = x.shape[0]
    tm = n if decode else ROW_TILE
    out_shape = [jax.ShapeDtypeStruct((n, D_MODEL), F32)]
    out_specs = [_rows(tm, D_MODEL)]
    if decode:
        out_shape.append(jax.ShapeDtypeStruct((n, A_INNER), F32))
        out_specs.append(_rows(tm, A_INNER))
    arrs, specs = zip(*(_layer(w) for w in (win, lg, lb, ws, bs, wout)))
    return pl.pallas_call(
        functools.partial(_mixa_body, decode=decode),
        grid=(n // tm,), in_specs=[_rows(tm, D_MODEL)] + list(specs), out_specs=out_specs, out_shape=out_shape,
        scratch_shapes=[pltpu.VMEM((tm, A_INNER), F32)],
        compiler_params=_cparams("parallel"), name="mix_a_dec" if decode else "mix_a")(x, *arrs)


def _gdn_post(act, x, wz, wba, arow, dtrow, q_o, k_o, v_o, z_o, ba_o):
    for h in range(GDN_H):
        l = slice(h * GDN_D, (h + 1) * GDN_D)
        qh = act[:, l]
        q_o[:, l] = qh * lax.rsqrt(jnp.sum(qh * qh, axis=-1, keepdims=True) + NORM_EPS) * (GDN_D ** -0.5)
        kh = act[:, D_MODEL + h * GDN_D:D_MODEL + (h + 1) * GDN_D]
        k_o[:, l] = kh * lax.rsqrt(jnp.sum(kh * kh, axis=-1, keepdims=True) + NORM_EPS)
    v_o[...] = act[:, 2 * D_MODEL:]
    z_o[...] = _bdot(x, wz[...])
    lg = _bdot(x, wba[...])
    lane = lax.broadcasted_iota(jnp.int32, lg.shape, 1)
    ba_o[...] = jnp.where(lane < GDN_H, _sigmoid(lg), arow[...] * _softplus(lg + dtrow[...]))


def _gdn_pre_body(x_ref, xh_ref, wqkv, wz, wba, cw, arow, dtrow, q_o, k_o, v_o, z_o, ba_o, tail_o, ext,
                  *, tiles_per_seq):
    tm = x_ref.shape[0]
    x = x_ref[...]
    pre = _bdot(x, wqkv[...])
    halo = _bdot(xh_ref[...], wqkv[...])
    first = pl.program_id(0) % tiles_per_seq == 0
    ext[0:8, :] = jnp.where(first, 0.0, halo)
    ext[8:8 + tm, :] = pre
    tail_o[0] = pre[tm - 8:, :]
    conv = ext[pl.ds(5, tm), :] * cw[0:1, :]
    for j in range(1, GDN_CONV):
        conv = conv + ext[pl.ds(5 + j, tm), :] * cw[j:j + 1, :]
    _gdn_post(_silu(conv), x, wz, wba, arow, dtrow, q_o, k_o, v_o, z_o, ba_o)


def _gdn_pre_dec_body(x_ref, buf_ref, wqkv, wz, wba, cw, arow, dtrow, q_o, k_o, v_o, z_o, ba_o, nbuf_o, ext):
    n = x_ref.shape[0]
    nb = n // 4
    x = x_ref[...]
    ext[0:3 * nb, :] = buf_ref[...]
    ext[3 * nb:3 * nb + n, :] = _bdot(x, wqkv[...])
    nbuf_o[...] = ext[4 * nb:7 * nb, :]
    conv = ext[0:n, :] * cw[0:1, :]
    for j in range(1, GDN_CONV):
        conv = conv + ext[j * nb:j * nb + n, :] * cw[j:j + 1, :]
    _gdn_post(_silu(conv), x, wz, wba, arow, dtrow, q_o, k_o, v_o, z_o, ba_o)


def _gdn_pre(x, wqkv, wz, wba, cw, arow, dtrow, seq_len=None, conv_buf=None):
    n = x.shape[0]
    w_specs = [_resident(wqkv.shape), _resident(wz.shape), _resident(wba.shape), _resident(cw.shape),
               _resident((1, 128)), _resident((1, 128))]
    outs = [jax.ShapeDtypeStruct((n, D_MODEL), F32)] * 4 + [jax.ShapeDtypeStruct((n, 128), F32)]
    if conv_buf is None:
        tm = ROW_TILE
        tps = seq_len // tm
        nseq = n // seq_len
        return pl.pallas_call(
            functools.partial(_gdn_pre_body, tiles_per_seq=tps), grid=(n // tm,),
            in_specs=[_rows(tm, D_MODEL),
                      pl.BlockSpec((8, D_MODEL), lambda i: (jnp.maximum(i * (tm // 8) - 1, 0), 0))] + w_specs,
            out_specs=[_rows(tm, D_MODEL)] * 4 + [_rows(tm, 128),
                                                   pl.BlockSpec((1, 8, GDN_QKV), lambda i: (i // tps, 0, 0))],
            out_shape=outs + [jax.ShapeDtypeStruct((nseq, 8, GDN_QKV), F32)],
            scratch_shapes=[pltpu.VMEM((tm + 8, GDN_QKV), F32)],
            compiler_params=_cparams("arbitrary"), name="gdn_pre")(x, x, wqkv, wz, wba, cw, arow, dtrow)
    nb = n // 4
    return pl.pallas_call(
        _gdn_pre_dec_body, grid=(1,),
        in_specs=[_rows(n, D_MODEL), _rows(3 * nb, GDN_QKV)] + w_specs,
        out_specs=[_rows(n, D_MODEL)] * 4 + [_rows(n, 128), _rows(3 * nb, GDN_QKV)],
        out_shape=outs + [jax.ShapeDtypeStruct((3 * nb, GDN_QKV), F32)],
        scratch_shapes=[pltpu.VMEM((7 * nb, GDN_QKV), F32)],
        compiler_params=_cparams("arbitrary"), name="gdn_pre_dec")(x, conv_buf, wqkv, wz, wba, cw, arow, dtrow)


def _gdn_scan_body(q_ref, k_ref, v_ref, ba_ref, s0_ref, o_ref, s_ref):
    c = q_ref.shape[0]
    heads = range(GDN_H)

    @pl.when(pl.program_id(1) == 0)
    def _():
        s_ref[...] = s0_ref[...]

    ba = ba_ref[...]
    ri = lax.broadcasted_iota(jnp.int32, (c, c), 0)
    ci = lax.broadcasted_iota(jnp.int32, (c, c), 1)
    incl = ci <= ri
    strict = ci < ri
    eye = (ri == ci).astype(F32)
    gcum_all = _hdot(incl.astype(F32), ba)
    lanes = [slice(h * GDN_D, (h + 1) * GDN_D) for h in heads]
    gc, decay, kbeta, inv, pw = [], [], [], [], []
    for h in heads:
        kh = k_ref[:, lanes[h]]
        g = gcum_all[:, GDN_H + h:GDN_H + h + 1]
        g_row = jnp.sum(g * eye, axis=0, keepdims=True)
        d = jnp.where(incl, jnp.exp(jnp.where(incl, g - g_row, 0.0)), 0.0)
        kb = kh * ba[:, h:h + 1]
        a = jnp.where(strict, _bdot_nt(kb, kh) * d, 0.0)
        gc.append(g), decay.append(d), kbeta.append(kb), inv.append(eye - a), pw.append(a)
    pw = [_split(p) for p in pw]
    for _ in range(int(math.log2(c)) - 1):
        pw = [_split(_dot3(p, p)) for p in pw]
        inv = [t + _dot3(_split(t), p) for t, p in zip(inv, pw)]
    u = [_bdot(inv[h], v_ref[:, lanes[h]] * ba[:, h:h + 1]) for h in heads]
    w = [_bdot(inv[h], kbeta[h] * jnp.exp(gc[h])) for h in heads]
    qk = [jnp.where(incl, _bdot_nt(q_ref[:, lanes[h]], k_ref[:, lanes[h]]) * decay[h], 0.0) for h in heads]
    s = [s_ref[0, h] for h in heads]
    v_new = [u[h] - _bdot(w[h], s[h]) for h in heads]
    for h in heads:
        o_ref[:, lanes[h]] = _bdot(q_ref[:, lanes[h]] * jnp.exp(gc[h]), s[h]) + _bdot(qk[h], v_new[h])
    for h in heads:
        g_last = gc[h][c - 1:c, :]
        kd = k_ref[:, lanes[h]] * jnp.exp(g_last - gc[h])
        s_ref[0, h] = s[h] * jnp.exp(g_last) + _bdot_tn(kd, v_new[h])


def _gdn_scan(q, k, v, ba, s0, chunk):
    n = q.shape[0]
    nseq = s0.shape[0]
    nch = n // nseq // chunk
    row = lambda w: pl.BlockSpec((chunk, w), lambda b, c: (b * nch + c, 0))
    st = pl.BlockSpec((1, GDN_H, GDN_D, GDN_D), lambda b, c: (b, 0, 0, 0))
    return pl.pallas_call(
        _gdn_scan_body, grid=(nseq, nch),
        in_specs=[row(D_MODEL), row(D_MODEL), row(D_MODEL), row(128), st],
        out_specs=[row(D_MODEL), st],
        out_shape=[jax.ShapeDtypeStruct((n, D_MODEL), F32), jax.ShapeDtypeStruct(s0.shape, F32)],
        compiler_params=_cparams("parallel", "arbitrary"), name="gdn_scan")(q, k, v, ba, s0)


def _gdn_out_body(o_ref, z_ref, ng, wout, y_ref, t_ref):
    for h in range(GDN_H):
        l = slice(h * GDN_D, (h + 1) * GDN_D)
        oh = o_ref[:, l]
        t_ref[:, l] = oh * lax.rsqrt(jnp.mean(oh * oh, axis=-1, keepdims=True) + NORM_EPS)
    y_ref[...] = _bdot(t_ref[...] * ng[...] * _silu(z_ref[...]), wout[...])


def _gdn_out(o, z, ng, wout):
    n = o.shape[0]
    tm = ROW_TILE
    return pl.pallas_call(
        _gdn_out_body, grid=(n // tm,),
        in_specs=[_rows(tm, D_MODEL), _rows(tm, D_MODEL), _resident((1, D_MODEL)), _resident(wout.shape)],
        out_specs=_rows(tm, D_MODEL), out_shape=jax.ShapeDtypeStruct((n, D_MODEL), F32),
        scratch_shapes=[pltpu.VMEM((tm, D_MODEL), F32)],
        compiler_params=_cparams("parallel"), name="gdn_out")(o, z, ng, wout)


NSA_SCALE = NSA_DH ** -0.5
PAGE = 128


def _projt_body(x_ref, wt_ref, *o_refs):
    yt = _bdot_nt(wt_ref[...], x_ref[...])
    off = 0
    for o in o_refs:
        wdt = o.shape[1]
        o[0] = yt[off:off + wdt, :]
        off += wdt


def _projt(x, wt, widths, seq_len):
    n = x.shape[0]
    tm = ROW_TILE
    tps = seq_len // tm
    return pl.pallas_call(
        _projt_body, grid=(n // tm,),
        in_specs=[_rows(tm, D_MODEL), _resident(wt.shape)],
        out_specs=[pl.BlockSpec((1, wd, tm), lambda i: (i // tps, 0, i % tps)) for wd in widths],
        out_shape=[jax.ShapeDtypeStruct((n // seq_len, wd, seq_len), F32) for wd in widths],
        compiler_params=_cparams("parallel"), name="nsa_proj")(x, wt)


def _cmp_body(*refs, n_pages, group, paged):
    if paged:
        refs = refs[1:]
    srcs, (pet, perm, w1, w2, o_ref, flat) = refs[:-6], refs[-6:]
    n_blk = n_pages * (PAGE // NSA_L_CMP)
    slot = pl.program_id(0) % group
    half = NSA_L_CMP // 2
    for c in range(2):
        for gp in range(2):
            def slab(p):
                if paged:
                    return srcs[p][0, 0, c, 2 * gp:2 * gp + 2].reshape(2 * NSA_DH, PAGE) + pet[c]
                return srcs[0][0, (2 * c + gp) * 128:(2 * c + gp + 1) * 128, p * PAGE:(p + 1) * PAGE] + pet[c]

            for pp in range(n_pages // 2):
                x2 = jnp.concatenate([slab(2 * pp), slab(2 * pp + 1)], axis=1)
                y2 = _bdot(x2, perm[...])
                rows = pl.ds(pl.multiple_of(slot * n_blk + pp * 8, 8), 8)
                for hlf in range(2):
                    t = y2[:, hlf * PAGE:(hlf + 1) * PAGE].T
                    for l in range(half):
                        lane0 = (hlf * half + l) * 128
                        flat[2 * c + gp, rows, lane0:lane0 + 128] = t[l * 8:(l + 1) * 8, :]

    @pl.when(slot == group - 1)
    def _():
        for c in range(2):
            for gp in range(2):
                hid = _gelu(_bdot(flat[2 * c + gp], w1[c]))
                o_ref[c, :, gp * 128:(gp + 1) * 128] = _bdot(hid, w2[c])


def _compress(src, pet, perm, w1bd, w2bd, page_table=None):
    paged = page_table is not None
    if paged:
        nseq, n_pages = page_table.shape
        group = 4
        src_specs = [pl.BlockSpec((1, 1, 2, NSA_G, NSA_DH, PAGE),
                                  functools.partial(lambda b, pt, p: (0, pt[b, p], 0, 0, 0, 0), p=p))
                     for p in range(n_pages)]
        srcs = [src] * n_pages
        wmap = lambda nd: (lambda b, pt: (0,) * nd)
        omap = lambda b, pt: (0, b // group, 0)
    else:
        nseq, _, t = src.shape
        n_pages = t // PAGE
        group = 1
        src_specs = [pl.BlockSpec((1, 2 * NSA_G * NSA_DH, t), lambda b: (b, 0, 0))]
        srcs = [src]
        wmap = lambda nd: (lambda b: (0,) * nd)
        omap = lambda b: (0, b, 0)
    n_blk = n_pages * (PAGE // NSA_L_CMP)
    in_specs = src_specs + [pl.BlockSpec(a.shape, wmap(a.ndim), pipeline_mode=pl.Buffered(1))
                            for a in (pet, perm, w1bd, w2bd)]
    out_spec = pl.BlockSpec((2, group * n_blk, NSA_G * NSA_DH), omap)
    scratch = [pltpu.VMEM((4, group * n_blk, NSA_L_CMP * 128), F32)]
    out_shape = jax.ShapeDtypeStruct((2, nseq * n_blk, NSA_G * NSA_DH), F32)
    body = functools.partial(_cmp_body, n_pages=n_pages, group=group, paged=paged)
    if paged:
        gs = pltpu.PrefetchScalarGridSpec(num_scalar_prefetch=1, grid=(nseq,), in_specs=in_specs, out_specs=out_spec,
                                          scratch_shapes=scratch)
        return pl.pallas_call(body, grid_spec=gs, out_shape=out_shape, compiler_params=_cparams("arbitrary"),
                              name="nsa_compress_paged")(page_table, *srcs, pet, perm, w1bd, w2bd)
    return pl.pallas_call(body, grid=(nseq,), in_specs=in_specs, out_specs=out_spec, out_shape=out_shape,
                          scratch_shapes=scratch, compiler_params=_cparams("arbitrary"),
                          name="nsa_compress")(*srcs, pet, perm, w1bd, w2bd)


def _select(ps, jq, n_slc):
    n_p, w = ps.shape
    j = lax.broadcasted_iota(jnp.int32, (n_p, w), 0)
    forced = (j == 0) | (j == jq) | (j == jq - 1)
    sc = jnp.where(forced, NSA_FORCE, jnp.where(j > jq, -1.0, ps))
    sc = jnp.where(j >= n_slc, -2.0, sc)
    rank = jnp.zeros((n_p, w), F32)
    for i in range(n_slc):
        r = sc[i:i + 1, :]
        rank = rank + jnp.where((r > sc) | ((r == sc) & (j > i)), 1.0, 0.0)
    return jnp.where(rank < NSA_N_SEL, 1.0, 0.0)


def _softmax_keys(s, mask):
    m = jnp.max(s, axis=0, keepdims=True)
    e = jnp.exp(s - m)
    return jnp.where(mask, e / jnp.sum(e, axis=0, keepdims=True), 0.0)


def _q4(q_ref, g0, tq):
    return jnp.concatenate([q_ref[0, (g0 + hh) * NSA_DH:(g0 + hh + 1) * NSA_DH, :] for hh in range(NSA_HG)], axis=1)


def _cmp_attn_body(q_ref, kc_ref, vc_ref, b_ref, pair_ref, o_ref, sel_ref, *, n_slc):
    tq = q_ref.shape[-1]
    kc, vc = kc_ref[0], vc_ref[0]
    qpos = pl.program_id(1) * tq + lax.broadcasted_iota(jnp.int32, (1, tq), 1)
    for g in range(NSA_G):
        f = slice(g * NSA_DH, (g + 1) * NSA_DH)
        bias = jnp.concatenate([b_ref[0, g * NSA_HG + hh] for hh in range(NSA_HG)], axis=1)
        mask = bias > 0.5 * NEG_INF
        s = jnp.where(mask, _bdot(kc[:, f], _q4(q_ref, g * NSA_HG, tq)) * NSA_SCALE + bias, NEG_INF)
        p = _softmax_keys(s, mask)
        o = _bdot_tn(vc[:, f], p)
        ps = p[:, :tq]
        for hh in range(NSA_HG):
            o_ref[0, (g * NSA_HG + hh) * NSA_DH:(g * NSA_HG + hh + 1) * NSA_DH, :] = o[:, hh * tq:(hh + 1) * tq]
            if hh:
                ps = ps + p[:, hh * tq:(hh + 1) * tq]
        sel_ref[0, g] = _select(_hdot(pair_ref[...], ps), qpos // NSA_L_SLC, n_slc)


def _cmp_attn(qt, kc, vc, bias, pair, n_slc):
    b, _, t = qt.shape
    tq = ATT_TILE
    n_cmp = kc.shape[1]
    n_p = pair.shape[0]
    qs = pl.BlockSpec((1, D_MODEL, tq), lambda b_, i: (b_, 0, i))
    ks = pl.BlockSpec((1, n_cmp, NSA_G * NSA_DH), lambda b_, i: (b_, 0, 0))
    return pl.pallas_call(
        functools.partial(_cmp_attn_body, n_slc=n_slc), grid=(b, t // tq),
        in_specs=[qs, ks, ks, pl.BlockSpec((1, NSA_H, n_cmp, tq), lambda b_, i: (i, 0, 0, 0)),
                  pl.BlockSpec(pair.shape, lambda b_, i: (0, 0))],
        out_specs=[qs, pl.BlockSpec((1, NSA_G, n_p, tq), lambda b_, i: (b_, 0, 0, i))],
        out_shape=[jax.ShapeDtypeStruct(qt.shape, F32), jax.ShapeDtypeStruct((b, NSA_G, n_p, t), F32)],
        compiler_params=_cparams("parallel", "parallel"), name="nsa_cmp_attn")(qt, kc, vc, bias, pair)


def _sw_attn_body(q_ref, ks_ref, vs_ref, sel_ref, b_ref, kw_ref, vw_ref, os_ref, ow_ref, negs, m_s, acc_s):
    tq = SW_TILE
    lanes = NSA_HG * tq
    n_tab = b_ref.shape[0]
    qt = pl.program_id(2)
    q0 = qt * tq
    q = (_q4(q_ref, 0, tq) * NSA_SCALE).astype(BF16)
    negs[...] = (jnp.concatenate([sel_ref[0, 0]] * NSA_HG, axis=1) - 1.0) * (-NEG_INF)
    ones = jnp.ones((8, n_tab), BF16)

    def sel_neg(k0, size):
        rows = [jnp.broadcast_to(negs[pl.ds(k0 // NSA_L_SLC + i, 1), :], (NSA_L_SLC, lanes))
                for i in range(size // NSA_L_SLC)]
        return jnp.concatenate(rows, axis=0)

    def scores(k_ref, k0, size, add):
        return _bdot_tn(k_ref[0, :, pl.ds(k0, size)], q) + add

    def update(s, v_ref, k0, size):
        vt = jnp.concatenate([v_ref[0, :, pl.ds(k0, size)].astype(BF16), ones[:, :size]], axis=0)
        m_old = m_s[...]
        m_new = jnp.maximum(m_old, jnp.max(s, axis=0, keepdims=True))
        p = jnp.exp(s - m_new).astype(BF16)
        acc_s[...] = jnp.exp(m_old - m_new) * acc_s[...] + jnp.dot(vt, p, preferred_element_type=F32)
        m_s[...] = m_new

    def near(k_ref, v_ref, size, masked):
        k0 = pl.multiple_of(q0 + tq - size, tq)
        add = b_ref[n_tab - size:, :]
        if masked:
            add = add + sel_neg(k0, size)
        update(scores(k_ref, k0, size, add), v_ref, k0, size)

    def init():
        m_s[...] = jnp.full(m_s.shape, NEG_INF, F32)
        acc_s[...] = jnp.zeros(acc_s.shape, F32)

    def write(o_ref):
        acc = acc_s[...]
        l = acc[NSA_DH:NSA_DH + 1, :]
        o = acc[:NSA_DH, :] / jnp.where(l > 0.0, l, 1.0)
        for hh in range(NSA_HG):
            o_ref[0, hh * NSA_DH:(hh + 1) * NSA_DH, :] = o[:, hh * tq:(hh + 1) * tq]

    init()
    pl.when(qt == 0)(lambda: near(ks_ref, vs_ref, tq, True))
    pl.when(qt >= 1)(lambda: near(ks_ref, vs_ref, 2 * tq, True))
    n_far = jnp.maximum(qt - 1, 0) * tq
    n_big = n_far // FAR_TILE

    def far_pair(i, carry):
        ka = pl.multiple_of(2 * i * FAR_TILE, FAR_TILE)
        kb = pl.multiple_of((2 * i + 1) * FAR_TILE, FAR_TILE)
        sa = scores(ks_ref, ka, FAR_TILE, sel_neg(ka, FAR_TILE))
        sb = scores(ks_ref, kb, FAR_TILE, sel_neg(kb, FAR_TILE))
        update(sa, vs_ref, ka, FAR_TILE)
        update(sb, vs_ref, kb, FAR_TILE)
        return carry

    lax.fori_loop(0, n_big // 2, far_pair, 0)

    def far_one(k0, size):
        update(scores(ks_ref, k0, size, sel_neg(k0, size)), vs_ref, k0, size)

    pl.when(n_big % 2 == 1)(lambda: far_one(pl.multiple_of((n_big - 1) * FAR_TILE, FAR_TILE), FAR_TILE))
    pl.when(n_far % FAR_TILE != 0)(lambda: far_one(pl.multiple_of(n_big * FAR_TILE, tq), tq))
    write(os_ref)

    init()
    for j in range(1, n_tab // tq):
        pl.when(qt == j - 1)(functools.partial(near, kw_ref, vw_ref, j * tq, False))
    pl.when(qt >= n_tab // tq - 1)(lambda: near(kw_ref, vw_ref, n_tab, False))
    write(ow_ref)


def _sw_attn(qt, kvt, wint, sel, btab):
    b, _, t = qt.shape
    tq = SW_TILE
    lanes = NSA_HG * tq
    qs = pl.BlockSpec((1, NSA_HG * NSA_DH, tq), lambda b_, g_, i: (b_, g_, i))
    feat = lambda blk0: pl.BlockSpec((1, NSA_DH, t), lambda b_, g_, i: (b_, blk0 + g_, 0))
    return pl.pallas_call(
        _sw_attn_body, grid=(b, NSA_G, t // tq),
        in_specs=[qs, feat(2 * NSA_G), feat(3 * NSA_G),
                  pl.BlockSpec((1, 1, sel.shape[2], tq), lambda b_, g_, i: (b_, g_, 0, i)),
                  pl.BlockSpec((btab.shape[0], lanes), lambda b_, g_, i: (0, g_)),
                  feat(0), feat(NSA_G)],
        out_specs=[qs, qs],
        out_shape=[jax.ShapeDtypeStruct(qt.shape, F32)] * 2,
        scratch_shapes=[pltpu.VMEM((sel.shape[2], lanes), F32), pltpu.VMEM((1, lanes), F32),
                        pltpu.VMEM((NSA_DH + 8, lanes), F32)],
        compiler_params=_cparams("parallel", "parallel", "arbitrary"),
        name="nsa_sw_attn")(qt, kvt, kvt, sel, btab, wint, wint)


def _attend(q, kts, vts, adds):
    s = [_bdot_tn(kt, q) + add for kt, add in zip(kts, adds)]
    m = functools.reduce(jnp.maximum, [jnp.max(x, axis=0, keepdims=True) for x in s])
    p = [jnp.exp(x - m) for x in s]
    l = functools.reduce(jnp.add, [jnp.sum(x, axis=0, keepdims=True) for x in p])
    acc = functools.reduce(jnp.add, [_bdot(vt, x) for vt, x in zip(vts, p)])
    return acc / l


def _dec_attn_body(pt_ref, *refs, n_pages, n_slc, q_pos0, tq):
    pages = refs[:n_pages]
    (q_ref, kc_ref, vc_ref, bc_ref, pair_ref, hsum_ref, knew_ref, vnew_ref, et_ref, bs_ref, win_ref, kwnew_ref,
     vwnew_ref, bw_ref, oc_ref, os_ref, ow_ref) = refs[n_pages:]
    feats, lanes = q_ref.shape[1:]
    q = (q_ref[0] * NSA_SCALE).astype(BF16)

    bias = bc_ref[...]
    mask = bias > 0.5 * NEG_INF
    p = _softmax_keys(jnp.where(mask, _bdot(kc_ref[0], q) + bias, NEG_INF), mask)
    oc_ref[0] = _bdot_tn(vc_ref[0], p)
    ps = _hdot(pair_ref[...], _hdot(p, hsum_ref[...]))
    lane = lax.broadcasted_iota(jnp.int32, (1, lanes), 1)
    sel = _select(ps, (q_pos0 + lane % tq) // NSA_L_SLC, n_slc).astype(BF16)

    kts = [pages[j][0, 0, 0].reshape(feats, PAGE) for j in range(n_pages)] + [knew_ref[0]]
    vts = [pages[j][0, 0, 1].reshape(feats, PAGE) for j in range(n_pages)] + [vnew_ref[0]]
    adds = [bs_ref[j] + (jnp.dot(et_ref[j], sel, preferred_element_type=F32) - 1.0) * (-NEG_INF)
            for j in range(n_pages + 1)]
    os_ref[0] = _attend(q, kts, vts, adds)

    n_wt = win_ref.shape[-1] // PAGE
    kts = [win_ref[0, 0, :, j * PAGE:(j + 1) * PAGE] for j in range(n_wt)] + [kwnew_ref[0]]
    vts = [win_ref[0, 1, :, j * PAGE:(j + 1) * PAGE] for j in range(n_wt)] + [vwnew_ref[0]]
    ow_ref[0] = _attend(q, kts, vts, [bw_ref[j] for j in range(n_wt + 1)])


def _dec_attn(page_table, cache_t, qbd, kc, vc, bias_c, pair, hsum, knew, vnew, et, bsel, win_t, kwnew, vwnew, bwin,
              n_slc, q_pos0, tq):
    nseq, n_pages = page_table.shape
    per_seq = lambda a: pl.BlockSpec((1,) + a.shape[1:], lambda b, pt: (b,) + (0,) * (a.ndim - 1))
    whole = lambda a: pl.BlockSpec(a.shape, lambda b, pt: (0,) * a.ndim, pipeline_mode=pl.Buffered(1))
    page_specs = [pl.BlockSpec((1, 1, 2, NSA_G, NSA_DH, PAGE),
                               functools.partial(lambda b, pt, p: (0, pt[b, p], 1, 0, 0, 0), p=p))
                  for p in range(n_pages)]
    gs = pltpu.PrefetchScalarGridSpec(
        num_scalar_prefetch=1, grid=(nseq,),
        in_specs=page_specs + [per_seq(qbd), per_seq(kc), per_seq(vc), whole(bias_c), whole(pair), whole(hsum),
                               per_seq(knew), per_seq(vnew), whole(et), whole(bsel), per_seq(win_t), per_seq(kwnew),
                               per_seq(vwnew), whole(bwin)],
        out_specs=[per_seq(qbd)] * 3)
    return pl.pallas_call(
        functools.partial(_dec_attn_body, n_pages=n_pages, n_slc=n_slc, q_pos0=q_pos0, tq=tq), grid_spec=gs,
        out_shape=[jax.ShapeDtypeStruct(qbd.shape, F32)] * 3, compiler_params=_cparams("parallel"),
        name="nsa_dec_attn")(page_table, *([cache_t] * n_pages), qbd, kc, vc, bias_c, pair, hsum, knew, vnew, et, bsel,
                             win_t, kwnew, vwnew, bwin)


def _nsa_out_body(x_ref, oc_ref, os_ref, ow_ref, wg, gb, wout, y_ref):
    gates = _sigmoid(_bdot_nt(wg[...], x_ref[...]) + gb[...])
    parts = []
    for h in range(NSA_H):
        r = slice(h * NSA_DH, (h + 1) * NSA_DH)
        parts.append(gates[h:h + 1, :] * oc_ref[0, r, :] + gates[NSA_H + h:NSA_H + h + 1, :] * os_ref[0, r, :]
                     + gates[2 * NSA_H + h:2 * NSA_H + h + 1, :] * ow_ref[0, r, :])
    y_ref[...] = _bdot_tn(jnp.concatenate(parts, axis=0), wout[...])


def _nsa_out(x, oc, osl, ow, wg, gb, wout):
    n = x.shape[0]
    tm = ROW_TILE
    tps = oc.shape[2] // tm
    col = pl.BlockSpec((1, D_MODEL, tm), lambda i: (i // tps, 0, i % tps))
    return pl.pallas_call(
        _nsa_out_body, grid=(n // tm,),
        in_specs=[_rows(tm, D_MODEL), col, col, col, _resident(wg.shape), _resident(gb.shape), _resident(wout.shape)],
        out_specs=_rows(tm, D_MODEL), out_shape=jax.ShapeDtypeStruct((n, D_MODEL), F32),
        compiler_params=_cparams("parallel"), name="nsa_out")(x, oc, osl, ow, wg, gb, wout)


def _t5_bucket(dist):
    n = jnp.maximum(dist, 0)
    max_exact = NUM_BUCKETS // 2
    nf = jnp.maximum(n, 1).astype(F32)
    large = max_exact + (jnp.log(nf / max_exact) / math.log(MAX_DISTANCE / max_exact)
                         * (NUM_BUCKETS - max_exact)).astype(jnp.int32)
    return jnp.where(n < max_exact, n, jnp.minimum(large, NUM_BUCKETS - 1))


def _bias_table(tab, dist, valid, head_major=False):
    onehot = (_t5_bucket(dist)[..., None] == jnp.arange(NUM_BUCKETS)).astype(F32)
    bias = jnp.einsum("...k,kh->...h", onehot, tab.astype(F32), precision=HIGHEST)
    bias = jnp.where(valid[..., None], bias, NEG_INF)
    if head_major:
        return jnp.moveaxis(bias, -1, -3)
    bias = jnp.swapaxes(bias, -1, -2)
    return bias.reshape(bias.shape[:-2] + (NSA_H * dist.shape[-1],))


def _nsa_weights(w_in, gate_b, pe, w1, w2, w_out):
    n_att = NSA_H * NSA_DH + 6 * NSA_G * NSA_DH
    eye2 = jnp.eye(2, dtype=F32)
    w1r = w1.reshape(2, NSA_L_CMP, NSA_DH, -1)
    w1bd = jnp.einsum("cldj,pq->clpdqj", w1r, eye2).reshape(2, NSA_L_CMP * 2 * NSA_DH, 2 * w1.shape[-1])
    w2bd = jnp.einsum("cjd,pq->cpjqd", w2, eye2).reshape(2, 2 * w2.shape[1], 2 * NSA_DH)
    pet = jnp.broadcast_to(pe.transpose(0, 2, 1)[:, None, :, None, :],
                           (2, 2, NSA_DH, PAGE // NSA_L_CMP, NSA_L_CMP)).reshape(2, 2 * NSA_DH, PAGE)
    pos = jnp.arange(2 * PAGE)
    dest = (pos % NSA_L_CMP) * (2 * PAGE // NSA_L_CMP) + pos // NSA_L_CMP
    perm = (dest[:, None] == pos[None, :]).astype(BF16)
    return dict(perm=perm, wt_in=w_in[:, :n_att].T.astype(BF16), w_gate=w_in[:, n_att:].T.astype(BF16),
                gate_b=gate_b.reshape(-1, 1), pet=pet, w1bd=w1bd.astype(BF16), w2bd=w2bd.astype(BF16),
                w_out=w_out.astype(BF16))


_NSA_WIDTHS = (NSA_H * NSA_DH, 4 * NSA_G * NSA_DH, 2 * NSA_G * NSA_DH)


def _pair_matrix(n_cmp, n_slc_p):
    return (jnp.arange(n_cmp)[None, :] // (NSA_L_SLC // NSA_L_CMP) == jnp.arange(n_slc_p)[:, None]).astype(F32)


def _block_expand(n_tiles, n_slc_p):
    et = jnp.arange(n_tiles * ATT_TILE)[:, None] // NSA_L_SLC == jnp.arange(n_slc_p)[None, :]
    return et.astype(BF16).reshape(n_tiles, ATT_TILE, n_slc_p)


def _nsa_prompt(x, seq_len, w, tab):
    nseq = x.shape[0] // seq_len
    tq = ATT_TILE
    nq = seq_len // tq
    n_cmp = seq_len // NSA_L_CMP
    n_slc = -(-seq_len // NSA_L_SLC)
    n_slc_p = -(-n_slc // 8) * 8
    qt, kvt, wint = _projt(x, w["wt_in"], _NSA_WIDTHS, seq_len)
    cmp = _compress(kvt, w["pet"], w["perm"], w["w1bd"], w["w2bd"]).reshape(2, nseq, n_cmp, NSA_G * NSA_DH)

    qpos = (jnp.arange(nq, dtype=jnp.int32) * tq)[:, None, None] + jnp.arange(tq, dtype=jnp.int32)[None, None, :]
    dist = qpos - (jnp.arange(n_cmp, dtype=jnp.int32) * NSA_L_CMP + NSA_L_CMP - 1)[None, :, None]
    oc, sel = _cmp_attn(qt, cmp[0], cmp[1], _bias_table(tab, dist, dist >= 0, head_major=True),
                        _pair_matrix(n_cmp, n_slc_p), n_slc)

    n_tab = NSA_WINDOW + SW_TILE
    d_t = (n_tab - SW_TILE + jnp.arange(SW_TILE, dtype=jnp.int32))[None, :] - jnp.arange(n_tab, dtype=jnp.int32)[:, None]
    btab = _bias_table(tab, d_t, (d_t >= 0) & (d_t < NSA_WINDOW))
    bfar = _bias_table(tab, jnp.full((1, SW_TILE), 4 * MAX_DISTANCE, jnp.int32), jnp.ones((1, SW_TILE), bool))
    osl, ow = _sw_attn(qt, kvt, wint, sel, jnp.where(btab > 0.5 * NEG_INF, btab - bfar, NEG_INF))
    y = _nsa_out(x, oc, osl, ow, w["w_gate"], w["gate_b"], w["w_out"])
    kv_new = kvt.reshape(nseq, 4, NSA_G, NSA_DH, seq_len).transpose(0, 4, 1, 2, 3)
    win_new = wint.reshape(nseq, 2, NSA_G, NSA_DH, seq_len).transpose(0, 4, 1, 2, 3)
    return y, kv_new, win_new


def _nsa_decode(x, n_tok, q_pos0, cache, win_past, page_table, w, tab):
    n = x.shape[0]
    nseq = n // n_tok
    tq = 8
    lanes = NSA_H * tq
    feats = NSA_G * NSA_DH
    n_pages = page_table.shape[1]
    past = n_pages * PAGE
    n_cmp = (past + n_tok) // NSA_L_CMP
    n_slc = -(-(past + n_tok) // NSA_L_SLC)
    n_slc_p = -(-n_slc // 8) * 8
    cache_t = cache.transpose(0, 2, 3, 4, 1)[None]
    win_t = win_past.transpose(0, 2, 3, 4, 1).reshape(nseq, 2, feats, win_past.shape[1])
    qt, kvt, wint = _projt(x, w["wt_in"], _NSA_WIDTHS, n)

    cmp = _compress(cache_t, w["pet"], w["perm"], w["w1bd"], w["w2bd"], page_table).reshape(2, nseq, n_cmp, feats)

    qg = qt[0].reshape(NSA_G, NSA_HG, NSA_DH, n_tok, nseq).transpose(4, 0, 2, 1, 3)
    qg = jnp.pad(qg, ((0, 0),) * 4 + ((0, tq - n_tok),)).reshape(nseq, NSA_G, NSA_DH, NSA_HG * tq)
    qbd = jnp.einsum("bgdl,gh->bgdhl", qg, jnp.eye(NSA_G, dtype=F32)).reshape(nseq, feats, lanes)
    new_t = lambda a: jnp.pad(a.reshape(feats, n_tok, nseq).transpose(2, 0, 1), ((0, 0), (0, 0), (0, PAGE - n_tok)))
    kv_new = kvt[0].reshape(4, NSA_G, NSA_DH, n_tok, nseq)
    win_new = wint[0].reshape(2, NSA_G, NSA_DH, n_tok, nseq)

    lane = jnp.arange(lanes, dtype=jnp.int32)
    qpos = q_pos0 + jnp.arange(tq, dtype=jnp.int32)[None, None, :]

    def bias_of(kpos, extra):
        d = qpos - kpos[:, :, None]
        return _bias_table(tab, d, (d >= 0) & extra(d, kpos[:, :, None]))

    tile_pos = lambda nt: (jnp.arange(nt, dtype=jnp.int32) * PAGE)[:, None] + jnp.arange(PAGE, dtype=jnp.int32)[None, :]
    kend = (jnp.arange(n_cmp, dtype=jnp.int32) * NSA_L_CMP + NSA_L_CMP - 1)[None, :]
    bias_c = bias_of(kend, lambda d, k: True)[0]
    bsel = bias_of(tile_pos(n_pages + 1), lambda d, k: k < past + n_tok)
    n_wt = win_past.shape[1] // PAGE
    wpos = jnp.concatenate([q_pos0 - n_wt * PAGE + tile_pos(n_wt), q_pos0 + tile_pos(1)])
    bwin = bias_of(wpos, lambda d, k: (d < NSA_WINDOW) & (k >= 0) & (k < q_pos0 + n_tok))
    hsum = ((lane[:, None] // (NSA_HG * tq) == lane[None, :] // (NSA_HG * tq))
            & (lane[:, None] % tq == lane[None, :] % tq)).astype(F32)

    outs = _dec_attn(page_table, cache_t, qbd, cmp[0], cmp[1], bias_c, _pair_matrix(n_cmp, n_slc_p), hsum,
                     new_t(kv_new[2]), new_t(kv_new[3]), _block_expand(n_pages + 1, n_slc_p), bsel, win_t,
                     new_t(win_new[0]), new_t(win_new[1]), bwin, n_slc, q_pos0, tq)

    def own_group(o):
        o = o.reshape(nseq, NSA_G, NSA_DH, NSA_G, NSA_HG, tq)
        o = jnp.diagonal(o, axis1=1, axis2=3)[:, :, :, :n_tok]
        return o.transpose(4, 2, 1, 3, 0).reshape(1, D_MODEL, n)

    y = _nsa_out(x, *(own_group(o) for o in outs), w["w_gate"], w["gate_b"], w["w_out"])
    return y, kv_new.transpose(4, 3, 0, 1, 2), win_new.transpose(4, 3, 0, 1, 2)


def kernel(x_prompt, x_sample, state_gdn_s, state_gdn_conv, cache_nsa_kv, cache_nsa_win, page_table, p_prompt,
           p_sample, ln_g, ln_b, ffn_w_up, ffn_w_down, ple_w_gate, ple_w_proj, a_w_in, a_ln_g, a_ln_b, a_w_s, a_b_s,
           a_w_out, gdn_w_in, gdn_conv_w, gdn_a_log, gdn_dt_bias, gdn_norm_g, gdn_w_out, nsa_w_in, nsa_gate_b,
           nsa_cmp_pe, nsa_cmp_w1, nsa_cmp_w2, nsa_w_out, t5_bias_table):
    bp, tp, _ = x_prompt.shape
    bs, ts, _ = x_sample.shape
    bf = lambda a: a.astype(BF16)
    row = lambda a: a.reshape(1, -1)
    xp = x_prompt.reshape(bp * tp, D_MODEL)
    xs = x_sample.transpose(1, 0, 2).reshape(ts * bs, D_MODEL)
    pp = p_prompt.reshape(DEPTH, bp * tp, -1)
    ps = p_sample.transpose(0, 2, 1, 3).reshape(DEPTH, ts * bs, -1)
    w_up, w_down, w_gate, w_proj = bf(ffn_w_up), bf(ffn_w_down), bf(ple_w_gate), bf(ple_w_proj)
    lng, lnb = ln_g.reshape(DEPTH, 3, 1, D_MODEL), ln_b.reshape(DEPTH, 3, 1, D_MODEL)
    a_win, a_wout = bf(a_w_in), bf(a_w_out)
    a_lg, a_lb = a_ln_g.reshape(-1, 1, A_INNER), a_ln_b.reshape(-1, 1, A_INNER)
    outs = {k: [] for k in ("a_v", "s_p", "c_p", "s_s", "c_s", "kv_p", "win_p", "kv_s", "win_s")}
    ia = ib = ic = 0
    ffn1 = lambda i: ((w_up, (i, 0)), (w_down, (i, 0)), (lng, (i, 0)), (lnb, (i, 0)))
    xp, xs = _ffn(xp, *ffn1(0)), _ffn(xs, *ffn1(0))
    for i in range(DEPTH):
        kind = i % 3
        if kind == 0:
            causal = jnp.tril(jnp.ones((A_CHUNK, A_CHUNK), bool))
            ws = bf(jnp.where(causal, a_w_s[ia], 0.0))
            bexp = jnp.repeat(a_b_s[ia].T, 128, axis=1)
            shared = ((a_win, (ia,)), (a_lg, (ia,)), (a_lb, (ia,)))
            (yp,) = _mix_a(xp, *shared, ws, bexp, (a_wout, (ia,)), decode=False)
            ws_dec = jnp.repeat(a_w_s[ia][:, :ts, :ts].transpose(1, 2, 0).reshape(ts * ts, A_GROUPS), 128, axis=1)
            ys, v_rows = _mix_a(xs, *shared, ws_dec, bexp[:8], (a_wout, (ia,)), decode=True)
            outs["a_v"].append(v_rows.reshape(ts, bs, A_INNER).transpose(1, 0, 2))
            ia += 1
        elif kind == 1:
            w = gdn_w_in[ib]
            wqkv, wz = bf(w[:, :GDN_QKV]), bf(w[:, GDN_QKV:GDN_QKV + D_MODEL])
            wba = bf(jnp.pad(w[:, GDN_QKV + D_MODEL:], ((0, 0), (0, 128 - 2 * GDN_H))))
            arow = jnp.pad(-jnp.exp(gdn_a_log[ib]), (GDN_H, 128 - 2 * GDN_H)).reshape(1, 128)
            dtrow = jnp.pad(gdn_dt_bias[ib], (GDN_H, 128 - 2 * GDN_H)).reshape(1, 128)
            ng, wout = row(jnp.tile(gdn_norm_g[ib], GDN_H)), bf(gdn_w_out[ib])
            q, k, v, z, ba, tail = _gdn_pre(xp, wqkv, wz, wba, gdn_conv_w[ib], arow, dtrow, seq_len=tp)
            o, s_p = _gdn_scan(q, k, v, ba, jnp.zeros((bp, GDN_H, GDN_D, GDN_D), F32), GDN_CHUNK)
            yp = _gdn_out(o, z, ng, wout)
            outs["s_p"].append(s_p)
            outs["c_p"].append(tail[:, 8 - (GDN_CONV - 1):])
            buf_t = state_gdn_conv[ib].transpose(1, 0, 2).reshape((GDN_CONV - 1) * bs, GDN_QKV)
            q, k, v, z, ba, nbuf = _gdn_pre(xs, wqkv, wz, wba, gdn_conv_w[ib], arow, dtrow, conv_buf=buf_t)
            seq8 = lambda a: jnp.pad(a.reshape(ts, bs, -1).transpose(1, 0, 2), ((0, 0), (0, 8 - ts), (0, 0))).reshape(
                bs * 8, -1)
            o, s_s = _gdn_scan(seq8(q), seq8(k), seq8(v), seq8(ba), state_gdn_s[ib], 8)
            o = o.reshape(bs, 8, D_MODEL)[:, :ts].transpose(1, 0, 2).reshape(ts * bs, D_MODEL)
            ys = _gdn_out(o, z, ng, wout)
            outs["s_s"].append(s_s)
            outs["c_s"].append(nbuf.reshape(GDN_CONV - 1, bs, GDN_QKV).transpose(1, 0, 2))
            ib += 1
        else:
            w = _nsa_weights(nsa_w_in[ic], nsa_gate_b[ic], nsa_cmp_pe[ic], nsa_cmp_w1[ic], nsa_cmp_w2[ic], nsa_w_out[ic])
            yp, kvn_p, winn_p = _nsa_prompt(xp, tp, w, t5_bias_table)
            past_len = page_table.shape[1] * cache_nsa_kv.shape[2]
            ys, kvn_s, winn_s = _nsa_decode(xs, ts, past_len, cache_nsa_kv[ic], cache_nsa_win[ic], page_table, w,
                                            t5_bias_table)
            outs["kv_p"].append(kvn_p)
            outs["win_p"].append(winn_p[:, tp - min(NSA_WINDOW, tp):])
            outs["kv_s"].append(kvn_s)
            outs["win_s"].append(winn_s)
            ic += 1
        pre = lambda y: (y, (lng, (i, 1)), (lnb, (i, 1)))
        ple = lambda p: ((p, (i,)), (w_gate, (i,)), (w_proj, (i,)))
        half2 = ((w_up, (i, 1)), (w_down, (i, 1)), (lng, (i, 2)), (lnb, (i, 2)))
        chain = ffn1(i + 1) if i + 1 < DEPTH else None
        xp = _ffn(xp, *half2, pre=pre(yp), ple=ple(pp), chain=chain)
        xs = _ffn(xs, *half2, pre=pre(ys), ple=ple(ps), chain=chain)
    st = lambda k: jnp.stack(outs[k])
    return (xp.reshape(bp, tp, D_MODEL), xs.reshape(ts, bs, D_MODEL).transpose(1, 0, 2), st("a_v"), st("s_p"),
            st("c_p"), st("s_s"), st("c_s"), st("kv_p"), st("win_p"), st("kv_s"), st("win_s"))
```

```python
import functools
import math

import jax
import jax.numpy as jnp
from jax import lax
from jax.experimental import pallas as pl
from jax.experimental.pallas import tpu as pltpu

F32 = jnp.float32
BF16 = jnp.bfloat16
HIGHEST = lax.Precision.HIGHEST

D_MODEL = 1024
DEPTH = 4
ALPHA = (2 * DEPTH) ** 0.25
LN_EPS = 1e-5
NORM_EPS = 1e-6
D_FF = 2816
A_CHUNK = 128
A_INNER = 2 * D_MODEL
A_GROUPS = 16
GDN_H = 8
GDN_D = 128
GDN_QKV = 3 * D_MODEL
GDN_CONV = 4
GDN_CHUNK = 128
NSA_H = 16
NSA_DH = 64
NSA_G = 4
NSA_HG = 4
NSA_L_CMP = 32
NSA_L_SLC = 64
NSA_N_SEL = 16
NSA_WINDOW = 512
NSA_FORCE = 100.0
NEG_INF = -1e30
NUM_BUCKETS = 32
MAX_DISTANCE = 128

ROW_TILE = 256
FFN_SPLIT = 2
ATT_TILE = 128
SW_TILE = 256
FAR_TILE = 512
VMEM_LIMIT = 56 << 20


def _cparams(*sem):
    return pltpu.CompilerParams(dimension_semantics=sem, vmem_limit_bytes=VMEM_LIMIT)


def _resident(shape):
    nd = len(shape)
    return pl.BlockSpec(shape, lambda *_: (0,) * nd, pipeline_mode=pl.Buffered(1))


def _rows(tm, width):
    return pl.BlockSpec((tm, width), lambda i: (i, 0))


def _layer(w):
    arr, idx = w if isinstance(w, tuple) else (w, ())
    nd = arr.ndim - len(idx)
    return arr, pl.BlockSpec((None,) * len(idx) + arr.shape[len(idx):], lambda *_: tuple(idx) + (0,) * nd,
                             pipeline_mode=pl.Buffered(1))


def _bdot(a, b):
    return jnp.dot(a.astype(BF16), b.astype(BF16), preferred_element_type=F32)


def _bdot_nt(a, b):
    return lax.dot_general(a.astype(BF16), b.astype(BF16), (((1,), (1,)), ((), ())),
                           preferred_element_type=F32)


def _bdot_tn(a, b):
    return lax.dot_general(a.astype(BF16), b.astype(BF16), (((0,), (0,)), ((), ())),
                           preferred_element_type=F32)


def _hdot(a, b):
    return jnp.dot(a, b, precision=HIGHEST, preferred_element_type=F32)


def _split(a):
    hi = a.astype(BF16)
    return hi, (a - hi.astype(F32)).astype(BF16)


def _dot3(a, b):
    d = lambda x, y: jnp.dot(x, y, preferred_element_type=F32)
    return d(a[0], b[0]) + (d(a[0], b[1]) + d(a[1], b[0]))


def _sigmoid(x):
    return 1.0 / (1.0 + jnp.exp(-x))


def _silu(x):
    return x * _sigmoid(x)


def _gelu(x):
    return 0.5 * x * (1.0 + jnp.tanh(math.sqrt(2.0 / math.pi) * (x + 0.044715 * (x * x * x))))


def _softplus(x):
    return jnp.maximum(x, 0.0) + jnp.log(1.0 + jnp.exp(-jnp.abs(x)))


def _ln(y, g, b):
    mu = jnp.mean(y, axis=-1, keepdims=True)
    d = y - mu
    var = jnp.mean(d * d, axis=-1, keepdims=True)
    return d * lax.rsqrt(var + LN_EPS) * g + b


def _ffn_body(*refs, pre_norm, ple, chain):
    it = iter(refs)
    x_ref = next(it)
    if pre_norm:
        y_ref, g1, b1 = next(it), next(it), next(it)
    wu, wd, g2, b2 = next(it), next(it), next(it), next(it)
    if ple:
        p_ref, wg, wp = next(it), next(it), next(it)
    if chain:
        wu_n, wd_n, g_n, b_n = next(it), next(it), next(it), next(it)
    o_ref = next(it)

    def half_step(xs, wu, wd, g, b):
        hs = [_bdot(x, wu[...]) for x in xs]
        acts = [_silu(h[:, :D_FF]) * h[:, D_FF:] for h in hs]
        fs = [_bdot(a, wd[...]) for a in acts]
        return [_ln(ALPHA * x + 0.5 * f, g[...], b[...]) for x, f in zip(xs, fs)]

    tm = x_ref.shape[0]
    parts = [slice(k * tm // FFN_SPLIT, (k + 1) * tm // FFN_SPLIT) for k in range(FFN_SPLIT)]
    xs = [x_ref[r, :] for r in parts]
    if pre_norm:
        xs = [_ln(ALPHA * x + y_ref[r, :], g1[...], b1[...]) for x, r in zip(xs, parts)]
    xs = half_step(xs, wu, wd, g2, b2)
    if ple:
        xs = [x + _sigmoid(_bdot(x, wg[...])) * _bdot(p_ref[r, :], wp[...]) for x, r in zip(xs, parts)]
    if chain:
        xs = half_step(xs, wu_n, wd_n, g_n, b_n)
    for x, r in zip(xs, parts):
        o_ref[r, :] = x


def _ffn(x, wu, wd, g2, b2, pre=None, ple=None, chain=None):
    n = x.shape[0]
    tm = ROW_TILE
    args, specs = [x], [_rows(tm, D_MODEL)]

    def resident(*ws):
        for w in ws:
            arr, spec = _layer(w)
            args.append(arr)
            specs.append(spec)

    if pre is not None:
        args.append(pre[0])
        specs.append(_rows(tm, D_MODEL))
        resident(*pre[1:])
    resident(wu, wd, g2, b2)
    if ple is not None:
        p, idx = ple[0] if isinstance(ple[0], tuple) else (ple[0], ())
        args.append(p)
        specs.append(pl.BlockSpec((None,) * len(idx) + (tm, p.shape[-1]), lambda i: tuple(idx) + (i, 0)))
        resident(*ple[1:])
    if chain is not None:
        resident(*chain)
    return pl.pallas_call(
        functools.partial(_ffn_body, pre_norm=pre is not None, ple=ple is not None, chain=chain is not None),
        grid=(n // tm,), in_specs=specs, out_specs=_rows(tm, D_MODEL),
        out_shape=jax.ShapeDtypeStruct((n, D_MODEL), F32),
        compiler_params=_cparams("parallel"), name="ffn")(*args)


def _mixa_body(x_ref, win, lg, lb, ws, bs, wout, y_ref, *rest, decode):
    if decode:
        v_ref, s_ref = rest
    else:
        (s_ref,) = rest
    tm = x_ref.shape[0]
    if decode:
        h = _gelu(_bdot(x_ref[...], win[...]))
        u = h[:, :A_INNER]
        v = _ln(h[:, A_INNER:], lg[...], lb[...])
        v_ref[...] = v
        nb = tm // 4
        for t in range(4):
            acc = bs[t:t + 1, :]
            for s in range(t + 1):
                acc = acc + ws[4 * t + s:4 * t + s + 1, :] * v[s * nb:(s + 1) * nb, :]
            s_ref[t * nb:(t + 1) * nb, :] = acc
        y_ref[...] = _bdot(u * s_ref[...], wout[...])
    else:
        chunks = [slice(c * A_CHUNK, (c + 1) * A_CHUNK) for c in range(tm // A_CHUNK)]
        hs = [_gelu(_bdot(x_ref[r, :], win[...])) for r in chunks]
        vs = [_ln(h[:, A_INNER:], lg[...], lb[...]) for h in hs]
        for r, v in zip(chunks, vs):
            for g in range(A_GROUPS):
                l = slice(g * 128, (g + 1) * 128)
                s_ref[r, l] = jnp.dot(ws[g], v[:, l].astype(BF16), preferred_element_type=F32) + bs[:, l]
        for r, h in zip(chunks, hs):
            y_ref[r, :] = _bdot(h[:, :A_INNER] * s_ref[r, :], wout[...])


def _mix_a(x, win, lg, lb, ws, bs, wout, decode):
    n = x.shape[0]
    tm = n if decode else ROW_TILE
    out_shape = [jax.ShapeDtypeStruct((n, D_MODEL), F32)]
    out_specs = [_rows(tm, D_MODEL)]
    if decode:
        out_shape.append(jax.ShapeDtypeStruct((n, A_INNER), F32))
        out_specs.append(_rows(tm, A_INNER))
    arrs, specs = zip(*(_layer(w) for w in (win, lg, lb, ws, bs, wout)))
    return pl.pallas_call(
        functools.partial(_mixa_body, decode=decode),
        grid=(n // tm,), in_specs=[_rows(tm, D_MODEL)] + list(specs), out_specs=out_specs, out_shape=out_shape,
        scratch_shapes=[pltpu.VMEM((tm, A_INNER), F32)],
        compiler_params=_cparams("parallel"), name="mix_a_dec" if decode else "mix_a")(x, *arrs)


def _gdn_post(act, x, wz, wba, arow, dtrow, q_o, k_o, v_o, z_o, ba_o):
    for h in range(GDN_H):
        l = slice(h * GDN_D, (h + 1) * GDN_D)
        qh = act[:, l]
        q_o[:, l] = qh * lax.rsqrt(jnp.sum(qh * qh, axis=-1, keepdims=True) + NORM_EPS) * (GDN_D ** -0.5)
        kh = act[:, D_MODEL + h * GDN_D:D_MODEL + (h + 1) * GDN_D]
        k_o[:, l] = kh * lax.rsqrt(jnp.sum(kh * kh, axis=-1, keepdims=True) + NORM_EPS)
    v_o[...] = act[:, 2 * D_MODEL:]
    z_o[...] = _bdot(x, wz[...])
    lg = _bdot(x, wba[...])
    lane = lax.broadcasted_iota(jnp.int32, lg.shape, 1)
    ba_o[...] = jnp.where(lane < GDN_H, _sigmoid(lg), arow[...] * _softplus(lg + dtrow[...]))


def _gdn_pre_body(x_ref, xh_ref, wqkv, wz, wba, cw, arow, dtrow, q_o, k_o, v_o, z_o, ba_o, tail_o, ext,
                  *, tiles_per_seq):
    tm = x_ref.shape[0]
    x = x_ref[...]
    pre = _bdot(x, wqkv[...])
    halo = _bdot(xh_ref[...], wqkv[...])
    first = pl.program_id(0) % tiles_per_seq == 0
    ext[0:8, :] = jnp.where(first, 0.0, halo)
    ext[8:8 + tm, :] = pre
    tail_o[0] = pre[tm - 8:, :]
    conv = ext[pl.ds(5, tm), :] * cw[0:1, :]
    for j in range(1, GDN_CONV):
        conv = conv + ext[pl.ds(5 + j, tm), :] * cw[j:j + 1, :]
    _gdn_post(_silu(conv), x, wz, wba, arow, dtrow, q_o, k_o, v_o, z_o, ba_o)


def _gdn_pre_dec_body(x_ref, buf_ref, wqkv, wz, wba, cw, arow, dtrow, q_o, k_o, v_o, z_o, ba_o, nbuf_o, ext):
    n = x_ref.shape[0]
    nb = n // 4
    x = x_ref[...]
    ext[0:3 * nb, :] = buf_ref[...]
    ext[3 * nb:3 * nb + n, :] = _bdot(x, wqkv[...])
    nbuf_o[...] = ext[4 * nb:7 * nb, :]
    conv = ext[0:n, :] * cw[0:1, :]
    for j in range(1, GDN_CONV):
        conv = conv + ext[j * nb:j * nb + n, :] * cw[j:j + 1, :]
    _gdn_post(_silu(conv), x, wz, wba, arow, dtrow, q_o, k_o, v_o, z_o, ba_o)


def _gdn_pre(x, wqkv, wz, wba, cw, arow, dtrow, seq_len=None, conv_buf=None):
    n = x.shape[0]
    w_specs = [_resident(wqkv.shape), _resident(wz.shape), _resident(wba.shape), _resident(cw.shape),
               _resident((1, 128)), _resident((1, 128))]
    outs = [jax.ShapeDtypeStruct((n, D_MODEL), F32)] * 4 + [jax.ShapeDtypeStruct((n, 128), F32)]
    if conv_buf is None:
        tm = ROW_TILE
        tps = seq_len // tm
        nseq = n // seq_len
        return pl.pallas_call(
            functools.partial(_gdn_pre_body, tiles_per_seq=tps), grid=(n // tm,),
            in_specs=[_rows(tm, D_MODEL),
                      pl.BlockSpec((8, D_MODEL), lambda i: (jnp.maximum(i * (tm // 8) - 1, 0), 0))] + w_specs,
            out_specs=[_rows(tm, D_MODEL)] * 4 + [_rows(tm, 128),
                                                   pl.BlockSpec((1, 8, GDN_QKV), lambda i: (i // tps, 0, 0))],
            out_shape=outs + [jax.ShapeDtypeStruct((nseq, 8, GDN_QKV), F32)],
            scratch_shapes=[pltpu.VMEM((tm + 8, GDN_QKV), F32)],
            compiler_params=_cparams("arbitrary"), name="gdn_pre")(x, x, wqkv, wz, wba, cw, arow, dtrow)
    nb = n // 4
    return pl.pallas_call(
        _gdn_pre_dec_body, grid=(1,),
        in_specs=[_rows(n, D_MODEL), _rows(3 * nb, GDN_QKV)] + w_specs,
        out_specs=[_rows(n, D_MODEL)] * 4 + [_rows(n, 128), _rows(3 * nb, GDN_QKV)],
        out_shape=outs + [jax.ShapeDtypeStruct((3 * nb, GDN_QKV), F32)],
        scratch_shapes=[pltpu.VMEM((7 * nb, GDN_QKV), F32)],
        compiler_params=_cparams("arbitrary"), name="gdn_pre_dec")(x, conv_buf, wqkv, wz, wba, cw, arow, dtrow)


def _gdn_scan_body(q_ref, k_ref, v_ref, ba_ref, s0_ref, o_ref, s_ref):
    c = q_ref.shape[0]
    heads = range(GDN_H)

    @pl.when(pl.program_id(1) == 0)
    def _():
        s_ref[...] = s0_ref[...]

    ba = ba_ref[...]
    ri = lax.broadcasted_iota(jnp.int32, (c, c), 0)
    ci = lax.broadcasted_iota(jnp.int32, (c, c), 1)
    incl = ci <= ri
    strict = ci < ri
    eye = (ri == ci).astype(F32)
    gcum_all = _hdot(incl.astype(F32), ba)
    lanes = [slice(h * GDN_D, (h + 1) * GDN_D) for h in heads]
    gc, decay, kbeta, inv, pw = [], [], [], [], []
    for h in heads:
        kh = k_ref[:, lanes[h]]
        g = gcum_all[:, GDN_H + h:GDN_H + h + 1]
        g_row = jnp.sum(g * eye, axis=0, keepdims=True)
        d = jnp.where(incl, jnp.exp(jnp.where(incl, g - g_row, 0.0)), 0.0)
        kb = kh * ba[:, h:h + 1]
        a = jnp.where(strict, _bdot_nt(kb, kh) * d, 0.0)
        gc.append(g), decay.append(d), kbeta.append(kb), inv.append(eye - a), pw.append(a)
    pw = [_split(p) for p in pw]
    for _ in range(int(math.log2(c)) - 1):
        pw = [_split(_dot3(p, p)) for p in pw]
        inv = [t + _dot3(_split(t), p) for t, p in zip(inv, pw)]
    u = [_bdot(inv[h], v_ref[:, lanes[h]] * ba[:, h:h + 1]) for h in heads]
    w = [_bdot(inv[h], kbeta[h] * jnp.exp(gc[h])) for h in heads]
    qk = [jnp.where(incl, _bdot_nt(q_ref[:, lanes[h]], k_ref[:, lanes[h]]) * decay[h], 0.0) for h in heads]
    s = [s_ref[0, h] for h in heads]
    v_new = [u[h] - _bdot(w[h], s[h]) for h in heads]
    for h in heads:
        o_ref[:, lanes[h]] = _bdot(q_ref[:, lanes[h]] * jnp.exp(gc[h]), s[h]) + _bdot(qk[h], v_new[h])
    for h in heads:
        g_last = gc[h][c - 1:c, :]
        kd = k_ref[:, lanes[h]] * jnp.exp(g_last - gc[h])
        s_ref[0, h] = s[h] * jnp.exp(g_last) + _bdot_tn(kd, v_new[h])


def _gdn_scan(q, k, v, ba, s0, chunk):
    n = q.shape[0]
    nseq = s0.shape[0]
    nch = n // nseq // chunk
    row = lambda w: pl.BlockSpec((chunk, w), lambda b, c: (b * nch + c, 0))
    st = pl.BlockSpec((1, GDN_H, GDN_D, GDN_D), lambda b, c: (b, 0, 0, 0))
    return pl.pallas_call(
        _gdn_scan_body, grid=(nseq, nch),
        in_specs=[row(D_MODEL), row(D_MODEL), row(D_MODEL), row(128), st],
        out_specs=[row(D_MODEL), st],
        out_shape=[jax.ShapeDtypeStruct((n, D_MODEL), F32), jax.ShapeDtypeStruct(s0.shape, F32)],
        compiler_params=_cparams("parallel", "arbitrary"), name="gdn_scan")(q, k, v, ba, s0)


def _gdn_out_body(o_ref, z_ref, ng, wout, y_ref, t_ref):
    for h in range(GDN_H):
        l = slice(h * GDN_D, (h + 1) * GDN_D)
        oh = o_ref[:, l]
        t_ref[:, l] = oh * lax.rsqrt(jnp.mean(oh * oh, axis=-1, keepdims=True) + NORM_EPS)
    y_ref[...] = _bdot(t_ref[...] * ng[...] * _silu(z_ref[...]), wout[...])


def _gdn_out(o, z, ng, wout):
    n = o.shape[0]
    tm = ROW_TILE
    return pl.pallas_call(
        _gdn_out_body, grid=(n // tm,),
        in_specs=[_rows(tm, D_MODEL), _rows(tm, D_MODEL), _resident((1, D_MODEL)), _resident(wout.shape)],
        out_specs=_rows(tm, D_MODEL), out_shape=jax.ShapeDtypeStruct((n, D_MODEL), F32),
        scratch_shapes=[pltpu.VMEM((tm, D_MODEL), F32)],
        compiler_params=_cparams("parallel"), name="gdn_out")(o, z, ng, wout)


NSA_SCALE = NSA_DH ** -0.5
PAGE = 128
NEW_TILE = 16


def _projt_body(x_ref, wt_ref, *o_refs):
    yt = _bdot_nt(wt_ref[...], x_ref[...])
    off = 0
    for o in o_refs:
        wdt = o.shape[1]
        o[0] = yt[off:off + wdt, :]
        off += wdt


def _projt(x, wt, widths, seq_len):
    n = x.shape[0]
    tm = ROW_TILE
    tps = seq_len // tm
    return pl.pallas_call(
        _projt_body, grid=(n // tm,),
        in_specs=[_rows(tm, D_MODEL), _resident(wt.shape)],
        out_specs=[pl.BlockSpec((1, wd, tm), lambda i: (i // tps, 0, i % tps)) for wd in widths],
        out_shape=[jax.ShapeDtypeStruct((n // seq_len, wd, seq_len), F32) for wd in widths],
        compiler_params=_cparams("parallel"), name="nsa_proj")(x, wt)


def _cmp_body(*refs, n_pages, group, paged):
    if paged:
        refs = refs[1:]
    srcs, (pet, perm, w1, w2, o_ref, flat) = refs[:-6], refs[-6:]
    n_blk = n_pages * (PAGE // NSA_L_CMP)
    slot = pl.program_id(0) % group
    half = NSA_L_CMP // 2
    for c in range(2):
        for gp in range(2):
            def slab(p):
                if paged:
                    return srcs[p][0, 0, c, 2 * gp:2 * gp + 2].reshape(2 * NSA_DH, PAGE) + pet[c]
                return srcs[0][0, (2 * c + gp) * 128:(2 * c + gp + 1) * 128, p * PAGE:(p + 1) * PAGE] + pet[c]

            for pp in range(n_pages // 2):
                x2 = jnp.concatenate([slab(2 * pp), slab(2 * pp + 1)], axis=1)
                y2 = _bdot(x2, perm[...])
                rows = pl.ds(pl.multiple_of(slot * n_blk + pp * 8, 8), 8)
                for hlf in range(2):
                    t = y2[:, hlf * PAGE:(hlf + 1) * PAGE].T
                    for l in range(half):
                        lane0 = (hlf * half + l) * 128
                        flat[2 * c + gp, rows, lane0:lane0 + 128] = t[l * 8:(l + 1) * 8, :]

    @pl.when(slot == group - 1)
    def _():
        for c in range(2):
            for gp in range(2):
                hid = _gelu(_bdot(flat[2 * c + gp], w1[c]))
                o_ref[c, :, gp * 128:(gp + 1) * 128] = _bdot(hid, w2[c])


def _compress(src, pet, perm, w1bd, w2bd, page_table=None):
    paged = page_table is not None
    if paged:
        nseq, n_pages = page_table.shape
        group = 4
        src_specs = [pl.BlockSpec((1, 1, 2, NSA_G, NSA_DH, PAGE),
                                  functools.partial(lambda b, pt, p: (0, pt[b, p], 0, 0, 0, 0), p=p))
                     for p in range(n_pages)]
        srcs = [src] * n_pages
        wmap = lambda nd: (lambda b, pt: (0,) * nd)
        omap = lambda b, pt: (0, b // group, 0)
    else:
        nseq, _, t = src.shape
        n_pages = t // PAGE
        group = 1
        src_specs = [pl.BlockSpec((1, 2 * NSA_G * NSA_DH, t), lambda b: (b, 0, 0))]
        srcs = [src]
        wmap = lambda nd: (lambda b: (0,) * nd)
        omap = lambda b: (0, b, 0)
    n_blk = n_pages * (PAGE // NSA_L_CMP)
    in_specs = src_specs + [pl.BlockSpec(a.shape, wmap(a.ndim), pipeline_mode=pl.Buffered(1))
                            for a in (pet, perm, w1bd, w2bd)]
    out_spec = pl.BlockSpec((2, group * n_blk, NSA_G * NSA_DH), omap)
    scratch = [pltpu.VMEM((4, group * n_blk, NSA_L_CMP * 128), F32)]
    out_shape = jax.ShapeDtypeStruct((2, nseq * n_blk, NSA_G * NSA_DH), F32)
    body = functools.partial(_cmp_body, n_pages=n_pages, group=group, paged=paged)
    if paged:
        gs = pltpu.PrefetchScalarGridSpec(num_scalar_prefetch=1, grid=(nseq,), in_specs=in_specs, out_specs=out_spec,
                                          scratch_shapes=scratch)
        return pl.pallas_call(body, grid_spec=gs, out_shape=out_shape, compiler_params=_cparams("arbitrary"),
                              name="nsa_compress_paged")(page_table, *srcs, pet, perm, w1bd, w2bd)
    return pl.pallas_call(body, grid=(nseq,), in_specs=in_specs, out_specs=out_spec, out_shape=out_shape,
                          scratch_shapes=scratch, compiler_params=_cparams("arbitrary"),
                          name="nsa_compress")(*srcs, pet, perm, w1bd, w2bd)


def _select(ps, jq, n_slc):
    n_p, w = ps.shape
    j = lax.broadcasted_iota(jnp.int32, (n_p, w), 0)
    forced = (j == 0) | (j == jq) | (j == jq - 1)
    sc = jnp.where(forced, NSA_FORCE, jnp.where(j > jq, -1.0, ps))
    sc = jnp.where(j >= n_slc, -2.0, sc)
    rank = jnp.zeros((n_p, w), F32)
    for i in range(n_slc):
        r = sc[i:i + 1, :]
        rank = rank + jnp.where((r > sc) | ((r == sc) & (j > i)), 1.0, 0.0)
    return jnp.where(rank < NSA_N_SEL, 1.0, 0.0)


def _softmax_keys(s, mask):
    m = jnp.max(s, axis=0, keepdims=True)
    e = jnp.exp(s - m)
    return jnp.where(mask, e / jnp.sum(e, axis=0, keepdims=True), 0.0)


def _q4(q_ref, g0, tq):
    return jnp.concatenate([q_ref[0, (g0 + hh) * NSA_DH:(g0 + hh + 1) * NSA_DH, :] for hh in range(NSA_HG)], axis=1)


def _cmp_attn_body(q_ref, kc_ref, vc_ref, b_ref, pair_ref, o_ref, sel_ref, *, n_slc):
    tq = q_ref.shape[-1]
    kc, vc = kc_ref[0], vc_ref[0]
    qpos = pl.program_id(1) * tq + lax.broadcasted_iota(jnp.int32, (1, tq), 1)
    for g in range(NSA_G):
        f = slice(g * NSA_DH, (g + 1) * NSA_DH)
        bias = jnp.concatenate([b_ref[0, g * NSA_HG + hh] for hh in range(NSA_HG)], axis=1)
        mask = bias > 0.5 * NEG_INF
        s = jnp.where(mask, _bdot(kc[:, f], _q4(q_ref, g * NSA_HG, tq)) * NSA_SCALE + bias, NEG_INF)
        p = _softmax_keys(s, mask)
        o = _bdot_tn(vc[:, f], p)
        ps = p[:, :tq]
        for hh in range(NSA_HG):
            o_ref[0, (g * NSA_HG + hh) * NSA_DH:(g * NSA_HG + hh + 1) * NSA_DH, :] = o[:, hh * tq:(hh + 1) * tq]
            if hh:
                ps = ps + p[:, hh * tq:(hh + 1) * tq]
        sel_ref[0, g] = _select(_hdot(pair_ref[...], ps), qpos // NSA_L_SLC, n_slc)


def _cmp_attn(qt, kc, vc, bias, pair, n_slc):
    b, _, t = qt.shape
    tq = ATT_TILE
    n_cmp = kc.shape[1]
    n_p = pair.shape[0]
    qs = pl.BlockSpec((1, D_MODEL, tq), lambda b_, i: (b_, 0, i))
    ks = pl.BlockSpec((1, n_cmp, NSA_G * NSA_DH), lambda b_, i: (b_, 0, 0))
    return pl.pallas_call(
        functools.partial(_cmp_attn_body, n_slc=n_slc), grid=(b, t // tq),
        in_specs=[qs, ks, ks, pl.BlockSpec((1, NSA_H, n_cmp, tq), lambda b_, i: (i, 0, 0, 0)),
                  pl.BlockSpec(pair.shape, lambda b_, i: (0, 0))],
        out_specs=[qs, pl.BlockSpec((1, NSA_G, n_p, tq), lambda b_, i: (b_, 0, 0, i))],
        out_shape=[jax.ShapeDtypeStruct(qt.shape, F32), jax.ShapeDtypeStruct((b, NSA_G, n_p, t), F32)],
        compiler_params=_cparams("parallel", "parallel"), name="nsa_cmp_attn")(qt, kc, vc, bias, pair)


def _sw_attn_body(q_ref, ks_ref, vs_ref, sel_ref, b_ref, kw_ref, vw_ref, os_ref, ow_ref, negs, m_s, acc_s):
    tq = SW_TILE
    lanes = NSA_HG * tq
    n_tab = b_ref.shape[0]
    qt = pl.program_id(2)
    q0 = qt * tq
    q = (_q4(q_ref, 0, tq) * NSA_SCALE).astype(BF16)
    negs[...] = (jnp.concatenate([sel_ref[0, 0]] * NSA_HG, axis=1) - 1.0) * (-NEG_INF)
    ones = jnp.ones((8, n_tab), BF16)

    def sel_neg(k0, size):
        rows = [jnp.broadcast_to(negs[pl.ds(k0 // NSA_L_SLC + i, 1), :], (NSA_L_SLC, lanes))
                for i in range(size // NSA_L_SLC)]
        return jnp.concatenate(rows, axis=0)

    def scores(k_ref, k0, size, add):
        return _bdot_tn(k_ref[0, :, pl.ds(k0, size)], q) + add

    def update(s, v_ref, k0, size):
        vt = jnp.concatenate([v_ref[0, :, pl.ds(k0, size)].astype(BF16), ones[:, :size]], axis=0)
        m_old = m_s[...]
        m_new = jnp.maximum(m_old, jnp.max(s, axis=0, keepdims=True))
        p = jnp.exp(s - m_new).astype(BF16)
        acc_s[...] = jnp.exp(m_old - m_new) * acc_s[...] + jnp.dot(vt, p, preferred_element_type=F32)
        m_s[...] = m_new

    def near(k_ref, v_ref, size, masked):
        k0 = pl.multiple_of(q0 + tq - size, tq)
        add = b_ref[n_tab - size:, :]
        if masked:
            add = add + sel_neg(k0, size)
        update(scores(k_ref, k0, size, add), v_ref, k0, size)

    def init():
        m_s[...] = jnp.full(m_s.shape, NEG_INF, F32)
        acc_s[...] = jnp.zeros(acc_s.shape, F32)

    def write(o_ref):
        acc = acc_s[...]
        l = acc[NSA_DH:NSA_DH + 1, :]
        o = acc[:NSA_DH, :] / jnp.where(l > 0.0, l, 1.0)
        for hh in range(NSA_HG):
            o_ref[0, hh * NSA_DH:(hh + 1) * NSA_DH, :] = o[:, hh * tq:(hh + 1) * tq]

    init()
    pl.when(qt == 0)(lambda: near(ks_ref, vs_ref, tq, True))
    pl.when(qt >= 1)(lambda: near(ks_ref, vs_ref, 2 * tq, True))
    n_far = jnp.maximum(qt - 1, 0) * tq
    n_big = n_far // FAR_TILE

    def far_pair(i, carry):
        ka = pl.multiple_of(2 * i * FAR_TILE, FAR_TILE)
        kb = pl.multiple_of((2 * i + 1) * FAR_TILE, FAR_TILE)
        sa = scores(ks_ref, ka, FAR_TILE, sel_neg(ka, FAR_TILE))
        sb = scores(ks_ref, kb, FAR_TILE, sel_neg(kb, FAR_TILE))
        update(sa, vs_ref, ka, FAR_TILE)
        update(sb, vs_ref, kb, FAR_TILE)
        return carry

    lax.fori_loop(0, n_big // 2, far_pair, 0)

    def far_one(k0, size):
        update(scores(ks_ref, k0, size, sel_neg(k0, size)), vs_ref, k0, size)

    pl.when(n_big % 2 == 1)(lambda: far_one(pl.multiple_of((n_big - 1) * FAR_TILE, FAR_TILE), FAR_TILE))
    pl.when(n_far % FAR_TILE != 0)(lambda: far_one(pl.multiple_of(n_big * FAR_TILE, tq), tq))
    write(os_ref)

    init()
    for j in range(1, n_tab // tq):
        pl.when(qt == j - 1)(functools.partial(near, kw_ref, vw_ref, j * tq, False))
    pl.when(qt >= n_tab // tq - 1)(lambda: near(kw_ref, vw_ref, n_tab, False))
    write(ow_ref)


def _sw_attn(qt, kvt, wint, sel, btab):
    b, _, t = qt.shape
    tq = SW_TILE
    lanes = NSA_HG * tq
    qs = pl.BlockSpec((1, NSA_HG * NSA_DH, tq), lambda b_, g_, i: (b_, g_, i))
    feat = lambda blk0: pl.BlockSpec((1, NSA_DH, t), lambda b_, g_, i: (b_, blk0 + g_, 0))
    return pl.pallas_call(
        _sw_attn_body, grid=(b, NSA_G, t // tq),
        in_specs=[qs, feat(2 * NSA_G), feat(3 * NSA_G),
                  pl.BlockSpec((1, 1, sel.shape[2], tq), lambda b_, g_, i: (b_, g_, 0, i)),
                  pl.BlockSpec((btab.shape[0], lanes), lambda b_, g_, i: (0, g_)),
                  feat(0), feat(NSA_G)],
        out_specs=[qs, qs],
        out_shape=[jax.ShapeDtypeStruct(qt.shape, F32)] * 2,
        scratch_shapes=[pltpu.VMEM((sel.shape[2], lanes), F32), pltpu.VMEM((1, lanes), F32),
                        pltpu.VMEM((NSA_DH + 8, lanes), F32)],
        compiler_params=_cparams("parallel", "parallel", "arbitrary"),
        name="nsa_sw_attn")(qt, kvt, kvt, sel, btab, wint, wint)


def _attend(q, kts, vts, adds):
    s = [_bdot_tn(kt, q) + add for kt, add in zip(kts, adds)]
    m = functools.reduce(jnp.maximum, [jnp.max(x, axis=0, keepdims=True) for x in s])
    p = [jnp.exp(x - m) for x in s]
    l = functools.reduce(jnp.add, [jnp.sum(x, axis=0, keepdims=True) for x in p])
    acc = functools.reduce(jnp.add, [_bdot(vt, x) for vt, x in zip(vts, p)])
    return acc / l


def _own_group(o):
    w = o.shape[1] // NSA_G
    return jnp.concatenate([o[g * NSA_DH:(g + 1) * NSA_DH, g * w:(g + 1) * w] for g in range(NSA_G)], axis=0)


def _dec_attn_body(pt_ref, *refs, n_pages, n_slc, q_pos0, tq):
    pages = refs[:n_pages]
    (q_ref, kc_ref, vc_ref, bc_ref, pair_ref, hsum_ref, knew_ref, vnew_ref, et_ref, etn_ref, bs_ref, bsn_ref, win_ref,
     kwnew_ref, vwnew_ref, bw_ref, bwn_ref, oc_ref, os_ref, ow_ref) = refs[n_pages:]
    feats, lanes = q_ref.shape[1:]
    q = (q_ref[0] * NSA_SCALE).astype(BF16)

    bias = bc_ref[...]
    mask = bias > 0.5 * NEG_INF
    p = _softmax_keys(jnp.where(mask, _bdot(kc_ref[0], q) + bias, NEG_INF), mask)
    oc_ref[0] = _own_group(_bdot_tn(vc_ref[0], p))
    ps = _hdot(pair_ref[...], _hdot(p, hsum_ref[...]))
    lane = lax.broadcasted_iota(jnp.int32, (1, lanes), 1)
    sel = _select(ps, (q_pos0 + lane % tq) // NSA_L_SLC, n_slc).astype(BF16)

    kts = [pages[j][0, 0, 0].reshape(feats, PAGE) for j in range(n_pages)] + [knew_ref[0]]
    vts = [pages[j][0, 0, 1].reshape(feats, PAGE) for j in range(n_pages)] + [vnew_ref[0]]
    not_sel = lambda et: (jnp.dot(et, sel, preferred_element_type=F32) - 1.0) * (-NEG_INF)
    adds = [bs_ref[j] + not_sel(et_ref[j]) for j in range(n_pages)] + [bsn_ref[...] + not_sel(etn_ref[...])]
    os_ref[0] = _own_group(_attend(q, kts, vts, adds))

    n_wt = win_ref.shape[-1] // PAGE
    kts = [win_ref[0, 0, :, j * PAGE:(j + 1) * PAGE] for j in range(n_wt)] + [kwnew_ref[0]]
    vts = [win_ref[0, 1, :, j * PAGE:(j + 1) * PAGE] for j in range(n_wt)] + [vwnew_ref[0]]
    ow_ref[0] = _own_group(_attend(q, kts, vts, [bw_ref[j] for j in range(n_wt)] + [bwn_ref[...]]))


def _dec_attn(page_table, cache_t, qbd, kc, vc, bias_c, pair, hsum, knew, vnew, et, et_new, bsel, bsel_new, win_t,
              kwnew, vwnew, bwin, bwin_new, n_slc, q_pos0, tq):
    nseq, n_pages = page_table.shape
    per_seq = lambda a: pl.BlockSpec((1,) + a.shape[1:], lambda b, pt: (b,) + (0,) * (a.ndim - 1))
    whole = lambda a: pl.BlockSpec(a.shape, lambda b, pt: (0,) * a.ndim, pipeline_mode=pl.Buffered(1))
    page_specs = [pl.BlockSpec((1, 1, 2, NSA_G, NSA_DH, PAGE),
                               functools.partial(lambda b, pt, p: (0, pt[b, p], 1, 0, 0, 0), p=p))
                  for p in range(n_pages)]
    gs = pltpu.PrefetchScalarGridSpec(
        num_scalar_prefetch=1, grid=(nseq,),
        in_specs=page_specs + [per_seq(qbd), per_seq(kc), per_seq(vc), whole(bias_c), whole(pair), whole(hsum),
                               per_seq(knew), per_seq(vnew), whole(et), whole(et_new), whole(bsel), whole(bsel_new),
                               per_seq(win_t), per_seq(kwnew), per_seq(vwnew), whole(bwin), whole(bwin_new)],
        out_specs=[pl.BlockSpec((1, qbd.shape[1], qbd.shape[2] // NSA_G), lambda b, pt: (b, 0, 0))] * 3)
    return pl.pallas_call(
        functools.partial(_dec_attn_body, n_pages=n_pages, n_slc=n_slc, q_pos0=q_pos0, tq=tq), grid_spec=gs,
        out_shape=[jax.ShapeDtypeStruct((nseq, qbd.shape[1], qbd.shape[2] // NSA_G), F32)] * 3,
        compiler_params=_cparams("parallel"),
        name="nsa_dec_attn")(page_table, *([cache_t] * n_pages), qbd, kc, vc, bias_c, pair, hsum, knew, vnew, et, et_new,
                             bsel, bsel_new, win_t, kwnew, vwnew, bwin, bwin_new)


def _nsa_out_body(x_ref, oc_ref, os_ref, ow_ref, wg, gb, wout, y_ref):
    gates = _sigmoid(_bdot_nt(wg[...], x_ref[...]) + gb[...])
    parts = []
    for h in range(NSA_H):
        r = slice(h * NSA_DH, (h + 1) * NSA_DH)
        parts.append(gates[h:h + 1, :] * oc_ref[0, r, :] + gates[NSA_H + h:NSA_H + h + 1, :] * os_ref[0, r, :]
                     + gates[2 * NSA_H + h:2 * NSA_H + h + 1, :] * ow_ref[0, r, :])
    y_ref[...] = _bdot_tn(jnp.concatenate(parts, axis=0), wout[...])


def _nsa_out(x, oc, osl, ow, wg, gb, wout):
    n = x.shape[0]
    tm = ROW_TILE
    tps = oc.shape[2] // tm
    col = pl.BlockSpec((1, D_MODEL, tm), lambda i: (i // tps, 0, i % tps))
    return pl.pallas_call(
        _nsa_out_body, grid=(n // tm,),
        in_specs=[_rows(tm, D_MODEL), col, col, col, _resident(wg.shape), _resident(gb.shape), _resident(wout.shape)],
        out_specs=_rows(tm, D_MODEL), out_shape=jax.ShapeDtypeStruct((n, D_MODEL), F32),
        compiler_params=_cparams("parallel"), name="nsa_out")(x, oc, osl, ow, wg, gb, wout)


def _t5_bucket(dist):
    n = jnp.maximum(dist, 0)
    max_exact = NUM_BUCKETS // 2
    nf = jnp.maximum(n, 1).astype(F32)
    large = max_exact + (jnp.log(nf / max_exact) / math.log(MAX_DISTANCE / max_exact)
                         * (NUM_BUCKETS - max_exact)).astype(jnp.int32)
    return jnp.where(n < max_exact, n, jnp.minimum(large, NUM_BUCKETS - 1))


def _bias_table(tab, dist, valid, head_major=False):
    onehot = (_t5_bucket(dist)[..., None] == jnp.arange(NUM_BUCKETS)).astype(F32)
    bias = jnp.einsum("...k,kh->...h", onehot, tab.astype(F32), precision=HIGHEST)
    bias = jnp.where(valid[..., None], bias, NEG_INF)
    if head_major:
        return jnp.moveaxis(bias, -1, -3)
    bias = jnp.swapaxes(bias, -1, -2)
    return bias.reshape(bias.shape[:-2] + (NSA_H * dist.shape[-1],))


def _nsa_weights(w_in, gate_b, pe, w1, w2, w_out):
    n_att = NSA_H * NSA_DH + 6 * NSA_G * NSA_DH
    eye2 = jnp.eye(2, dtype=F32)
    w1r = w1.reshape(2, NSA_L_CMP, NSA_DH, -1)
    w1bd = jnp.einsum("cldj,pq->clpdqj", w1r, eye2).reshape(2, NSA_L_CMP * 2 * NSA_DH, 2 * w1.shape[-1])
    w2bd = jnp.einsum("cjd,pq->cpjqd", w2, eye2).reshape(2, 2 * w2.shape[1], 2 * NSA_DH)
    pet = jnp.broadcast_to(pe.transpose(0, 2, 1)[:, None, :, None, :],
                           (2, 2, NSA_DH, PAGE // NSA_L_CMP, NSA_L_CMP)).reshape(2, 2 * NSA_DH, PAGE)
    pos = jnp.arange(2 * PAGE)
    dest = (pos % NSA_L_CMP) * (2 * PAGE // NSA_L_CMP) + pos // NSA_L_CMP
    perm = (dest[:, None] == pos[None, :]).astype(BF16)
    return dict(perm=perm, wt_in=w_in[:, :n_att].T.astype(BF16), w_gate=w_in[:, n_att:].T.astype(BF16),
                gate_b=gate_b.reshape(-1, 1), pet=pet, w1bd=w1bd.astype(BF16), w2bd=w2bd.astype(BF16),
                w_out=w_out.astype(BF16))


_NSA_WIDTHS = (NSA_H * NSA_DH, 4 * NSA_G * NSA_DH, 2 * NSA_G * NSA_DH)


def _pair_matrix(n_cmp, n_slc_p):
    return (jnp.arange(n_cmp)[None, :] // (NSA_L_SLC // NSA_L_CMP) == jnp.arange(n_slc_p)[:, None]).astype(F32)


def _block_expand(n_tiles, n_slc_p):
    et = jnp.arange(n_tiles * ATT_TILE)[:, None] // NSA_L_SLC == jnp.arange(n_slc_p)[None, :]
    return et.astype(BF16).reshape(n_tiles, ATT_TILE, n_slc_p)


def _nsa_prompt(x, seq_len, w, tab):
    nseq = x.shape[0] // seq_len
    tq = ATT_TILE
    nq = seq_len // tq
    n_cmp = seq_len // NSA_L_CMP
    n_slc = -(-seq_len // NSA_L_SLC)
    n_slc_p = -(-n_slc // 8) * 8
    qt, kvt, wint = _projt(x, w["wt_in"], _NSA_WIDTHS, seq_len)
    cmp = _compress(kvt, w["pet"], w["perm"], w["w1bd"], w["w2bd"]).reshape(2, nseq, n_cmp, NSA_G * NSA_DH)

    qpos = (jnp.arange(nq, dtype=jnp.int32) * tq)[:, None, None] + jnp.arange(tq, dtype=jnp.int32)[None, None, :]
    dist = qpos - (jnp.arange(n_cmp, dtype=jnp.int32) * NSA_L_CMP + NSA_L_CMP - 1)[None, :, None]
    oc, sel = _cmp_attn(qt, cmp[0], cmp[1], _bias_table(tab, dist, dist >= 0, head_major=True),
                        _pair_matrix(n_cmp, n_slc_p), n_slc)

    n_tab = NSA_WINDOW + SW_TILE
    d_t = (n_tab - SW_TILE + jnp.arange(SW_TILE, dtype=jnp.int32))[None, :] - jnp.arange(n_tab, dtype=jnp.int32)[:, None]
    btab = _bias_table(tab, d_t, (d_t >= 0) & (d_t < NSA_WINDOW))
    bfar = _bias_table(tab, jnp.full((1, SW_TILE), 4 * MAX_DISTANCE, jnp.int32), jnp.ones((1, SW_TILE), bool))
    osl, ow = _sw_attn(qt, kvt, wint, sel, jnp.where(btab > 0.5 * NEG_INF, btab - bfar, NEG_INF))
    y = _nsa_out(x, oc, osl, ow, w["w_gate"], w["gate_b"], w["w_out"])
    kv_new = kvt.reshape(nseq, 4, NSA_G, NSA_DH, seq_len).transpose(0, 4, 1, 2, 3)
    win_new = wint.reshape(nseq, 2, NSA_G, NSA_DH, seq_len).transpose(0, 4, 1, 2, 3)
    return y, kv_new, win_new


def _nsa_decode(x, n_tok, q_pos0, cache, win_past, page_table, w, tab):
    n = x.shape[0]
    nseq = n // n_tok
    tq = 8
    lanes = NSA_H * tq
    feats = NSA_G * NSA_DH
    n_pages = page_table.shape[1]
    past = n_pages * PAGE
    n_cmp = (past + n_tok) // NSA_L_CMP
    n_slc = -(-(past + n_tok) // NSA_L_SLC)
    n_slc_p = -(-n_slc // 8) * 8
    cache_t = cache.transpose(0, 2, 3, 4, 1)[None]
    win_t = win_past.transpose(0, 2, 3, 4, 1).reshape(nseq, 2, feats, win_past.shape[1])
    qt, kvt, wint = _projt(x, w["wt_in"], _NSA_WIDTHS, n)

    cmp = _compress(cache_t, w["pet"], w["perm"], w["w1bd"], w["w2bd"], page_table).reshape(2, nseq, n_cmp, feats)

    qg = qt[0].reshape(NSA_G, NSA_HG, NSA_DH, n_tok, nseq).transpose(4, 0, 2, 1, 3)
    qg = jnp.pad(qg, ((0, 0),) * 4 + ((0, tq - n_tok),)).reshape(nseq, NSA_G, NSA_DH, NSA_HG * tq)
    qbd = jnp.einsum("bgdl,gh->bgdhl", qg, jnp.eye(NSA_G, dtype=F32)).reshape(nseq, feats, lanes)
    new_t = lambda a: jnp.pad(a.reshape(feats, n_tok, nseq).transpose(2, 0, 1), ((0, 0), (0, 0), (0, NEW_TILE - n_tok)))
    kv_new = kvt[0].reshape(4, NSA_G, NSA_DH, n_tok, nseq)
    win_new = wint[0].reshape(2, NSA_G, NSA_DH, n_tok, nseq)

    lane = jnp.arange(lanes, dtype=jnp.int32)
    qpos = q_pos0 + jnp.arange(tq, dtype=jnp.int32)[None, None, :]

    def bias_of(kpos, extra):
        d = qpos - kpos[:, :, None]
        return _bias_table(tab, d, (d >= 0) & extra(d, kpos[:, :, None]))

    tile_pos = lambda nt: (jnp.arange(nt, dtype=jnp.int32) * PAGE)[:, None] + jnp.arange(PAGE, dtype=jnp.int32)[None, :]
    new_pos = q_pos0 + jnp.arange(NEW_TILE, dtype=jnp.int32)[None, :]
    kend = (jnp.arange(n_cmp, dtype=jnp.int32) * NSA_L_CMP + NSA_L_CMP - 1)[None, :]
    bias_c = bias_of(kend, lambda d, k: True)[0]
    in_seq = lambda d, k: k < past + n_tok
    bsel, bsel_new = bias_of(tile_pos(n_pages), in_seq), bias_of(new_pos, in_seq)[0]
    n_wt = win_past.shape[1] // PAGE
    in_win = lambda d, k: (d < NSA_WINDOW) & (k >= 0) & (k < q_pos0 + n_tok)
    bwin, bwin_new = bias_of(q_pos0 - n_wt * PAGE + tile_pos(n_wt), in_win), bias_of(new_pos, in_win)[0]
    et_new = (new_pos[0][:, None] // NSA_L_SLC == jnp.arange(n_slc_p)[None, :]).astype(BF16)
    hsum = ((lane[:, None] // (NSA_HG * tq) == lane[None, :] // (NSA_HG * tq))
            & (lane[:, None] % tq == lane[None, :] % tq)).astype(F32)

    outs = _dec_attn(page_table, cache_t, qbd, cmp[0], cmp[1], bias_c, _pair_matrix(n_cmp, n_slc_p), hsum,
                     new_t(kv_new[2]), new_t(kv_new[3]), _block_expand(n_pages, n_slc_p), et_new, bsel, bsel_new, win_t,
                     new_t(win_new[0]), new_t(win_new[1]), bwin, bwin_new, n_slc, q_pos0, tq)

    def rows_of(o):
        o = o.reshape(nseq, NSA_G, NSA_DH, NSA_HG, tq)[..., :n_tok]
        return o.transpose(1, 3, 2, 4, 0).reshape(1, D_MODEL, n)

    y = _nsa_out(x, *(rows_of(o) for o in outs), w["w_gate"], w["gate_b"], w["w_out"])
    return y, kv_new.transpose(4, 3, 0, 1, 2), win_new.transpose(4, 3, 0, 1, 2)


def kernel(x_prompt, x_sample, state_gdn_s, state_gdn_conv, cache_nsa_kv, cache_nsa_win, page_table, p_prompt,
           p_sample, ln_g, ln_b, ffn_w_up, ffn_w_down, ple_w_gate, ple_w_proj, a_w_in, a_ln_g, a_ln_b, a_w_s, a_b_s,
           a_w_out, gdn_w_in, gdn_conv_w, gdn_a_log, gdn_dt_bias, gdn_norm_g, gdn_w_out, nsa_w_in, nsa_gate_b,
           nsa_cmp_pe, nsa_cmp_w1, nsa_cmp_w2, nsa_w_out, t5_bias_table):
    bp, tp, _ = x_prompt.shape
    bs, ts, _ = x_sample.shape
    bf = lambda a: a.astype(BF16)
    row = lambda a: a.reshape(1, -1)
    xp = x_prompt.reshape(bp * tp, D_MODEL)
    xs = x_sample.transpose(1, 0, 2).reshape(ts * bs, D_MODEL)
    pp = p_prompt.reshape(DEPTH, bp * tp, -1)
    ps = p_sample.transpose(0, 2, 1, 3).reshape(DEPTH, ts * bs, -1)
    w_up, w_down, w_gate, w_proj = bf(ffn_w_up), bf(ffn_w_down), bf(ple_w_gate), bf(ple_w_proj)
    lng, lnb = ln_g.reshape(DEPTH, 3, 1, D_MODEL), ln_b.reshape(DEPTH, 3, 1, D_MODEL)
    a_win, a_wout = bf(a_w_in), bf(a_w_out)
    a_lg, a_lb = a_ln_g.reshape(-1, 1, A_INNER), a_ln_b.reshape(-1, 1, A_INNER)
    outs = {k: [] for k in ("a_v", "s_p", "c_p", "s_s", "c_s", "kv_p", "win_p", "kv_s", "win_s")}
    ia = ib = ic = 0
    ffn1 = lambda i: ((w_up, (i, 0)), (w_down, (i, 0)), (lng, (i, 0)), (lnb, (i, 0)))
    xp, xs = _ffn(xp, *ffn1(0)), _ffn(xs, *ffn1(0))
    for i in range(DEPTH):
        kind = i % 3
        if kind == 0:
            causal = jnp.tril(jnp.ones((A_CHUNK, A_CHUNK), bool))
            ws = bf(jnp.where(causal, a_w_s[ia], 0.0))
            bexp = jnp.repeat(a_b_s[ia].T, 128, axis=1)
            shared = ((a_win, (ia,)), (a_lg, (ia,)), (a_lb, (ia,)))
            (yp,) = _mix_a(xp, *shared, ws, bexp, (a_wout, (ia,)), decode=False)
            ws_dec = jnp.repeat(a_w_s[ia][:, :ts, :ts].transpose(1, 2, 0).reshape(ts * ts, A_GROUPS), 128, axis=1)
            ys, v_rows = _mix_a(xs, *shared, ws_dec, bexp[:8], (a_wout, (ia,)), decode=True)
            outs["a_v"].append(v_rows.reshape(ts, bs, A_INNER).transpose(1, 0, 2))
            ia += 1
        elif kind == 1:
            w = gdn_w_in[ib]
            wqkv, wz = bf(w[:, :GDN_QKV]), bf(w[:, GDN_QKV:GDN_QKV + D_MODEL])
            wba = bf(jnp.pad(w[:, GDN_QKV + D_MODEL:], ((0, 0), (0, 128 - 2 * GDN_H))))
            arow = jnp.pad(-jnp.exp(gdn_a_log[ib]), (GDN_H, 128 - 2 * GDN_H)).reshape(1, 128)
            dtrow = jnp.pad(gdn_dt_bias[ib], (GDN_H, 128 - 2 * GDN_H)).reshape(1, 128)
            ng, wout = row(jnp.tile(gdn_norm_g[ib], GDN_H)), bf(gdn_w_out[ib])
            q, k, v, z, ba, tail = _gdn_pre(xp, wqkv, wz, wba, gdn_conv_w[ib], arow, dtrow, seq_len=tp)
            o, s_p = _gdn_scan(q, k, v, ba, jnp.zeros((bp, GDN_H, GDN_D, GDN_D), F32), GDN_CHUNK)
            yp = _gdn_out(o, z, ng, wout)
            outs["s_p"].append(s_p)
            outs["c_p"].append(tail[:, 8 - (GDN_CONV - 1):])
            buf_t = state_gdn_conv[ib].transpose(1, 0, 2).reshape((GDN_CONV - 1) * bs, GDN_QKV)
            q, k, v, z, ba, nbuf = _gdn_pre(xs, wqkv, wz, wba, gdn_conv_w[ib], arow, dtrow, conv_buf=buf_t)
            seq8 = lambda a: jnp.pad(a.reshape(ts, bs, -1).transpose(1, 0, 2), ((0, 0), (0, 8 - ts), (0, 0))).reshape(
                bs * 8, -1)
            o, s_s = _gdn_scan(seq8(q), seq8(k), seq8(v), seq8(ba), state_gdn_s[ib], 8)
            o = o.reshape(bs, 8, D_MODEL)[:, :ts].transpose(1, 0, 2).reshape(ts * bs, D_MODEL)
            ys = _gdn_out(o, z, ng, wout)
            outs["s_s"].append(s_s)
            outs["c_s"].append(nbuf.reshape(GDN_CONV - 1, bs, GDN_QKV).transpose(1, 0, 2))
            ib += 1
        else:
            w = _nsa_weights(nsa_w_in[ic], nsa_gate_b[ic], nsa_cmp_pe[ic], nsa_cmp_w1[ic], nsa_cmp_w2[ic], nsa_w_out[ic])
            yp, kvn_p, winn_p = _nsa_prompt(xp, tp, w, t5_bias_table)
            past_len = page_table.shape[1] * cache_nsa_kv.shape[2]
            ys, kvn_s, winn_s = _nsa_decode(xs, ts, past_len, cache_nsa_kv[ic], cache_nsa_win[ic], page_table, w,
                                            t5_bias_table)
            outs["kv_p"].append(kvn_p)
            outs["win_p"].append(winn_p[:, tp - min(NSA_WINDOW, tp):])
            outs["kv_s"].append(kvn_s)
            outs["win_s"].append(winn_s)
            ic += 1
        pre = lambda y: (y, (lng, (i, 1)), (lnb, (i, 1)))
        ple = lambda p: ((p, (i,)), (w_gate, (i,)), (w_proj, (i,)))
        half2 = ((w_up, (i, 1)), (w_down, (i, 1)), (lng, (i, 2)), (lnb, (i, 2)))
        chain = ffn1(i + 1) if i + 1 < DEPTH else None
        xp = _ffn(xp, *half2, pre=pre(yp), ple=ple(pp), chain=chain)
        xs = _ffn(xs, *half2, pre=pre(ys), ple=ple(ps), chain=chain)
    st = lambda k: jnp.stack(outs[k])
    return (xp.reshape(bp, tp, D_MODEL), xs.reshape(ts, bs, D_MODEL).transpose(1, 0, 2), st("a_v"), st("s_p"),
            st("c_p"), st("s_s"), st("c_s"), st("kv_p"), st("win_p"), st("kv_s"), st("win_s"))
```
